```python
import math
import jax, jax.numpy as jnp
from jax import lax
import numpy as np

D_MODEL = 1024
BATCH = 4
SEQ = 4096
DEPTH = 1
DEC_BATCH = 32
DEC_SEQ = 8
PAST_LEN = 16384
PAGE_SIZE = 128

HEAD_DIM = 64
SB_HEADS = 8
SB_WIDTH = SB_HEADS * HEAD_DIM
DF_HEADS = 4
DF_DH = HEAD_DIM
DF_VD = 2 * HEAD_DIM
DF_QK_WIDTH = DF_HEADS * 2 * DF_DH
DF_WIDTH = DF_HEADS * DF_VD
MIX_WIDTH = SB_WIDTH + DF_WIDTH
PROJ_WIDTH = 3 * SB_WIDTH + 2 * DF_QK_WIDTH + DF_WIDTH
SPLITS = [SB_WIDTH, 2 * SB_WIDTH, 3 * SB_WIDTH, 3 * SB_WIDTH + DF_QK_WIDTH,
          3 * SB_WIDTH + 2 * DF_QK_WIDTH]
D_FF = 2816
CONV_W = 3
Q_BLOCK = 128
EPS = 1e-6

kernel_name = "hymba_stickbreak_diffattn_convffn_step"


def rms_norm(x, g):
    xf = x.astype(jnp.float32)
    y = xf * lax.rsqrt(jnp.mean(xf * xf, axis=-1, keepdims=True) + EPS)
    return (y * g.astype(jnp.float32)).astype(x.dtype)


def alibi_slopes(n):
    return jnp.asarray(2.0 ** (-8.0 * np.arange(1, n + 1) / n), dtype=jnp.float32)


def sweep_query_blocks(block_fn, q_pos, q):
    b, tq = q.shape[0], q.shape[1]
    qb = min(Q_BLOCK, tq)
    nb = tq // qb
    qs = jnp.moveaxis(q.reshape(b, nb, qb, *q.shape[2:]), 1, 0)
    out = lax.map(lambda a: block_fn(a[0], a[1]), (q_pos.reshape(nb, qb), qs))
    out = jnp.moveaxis(out, 0, 1)
    return out.reshape(b, tq, *out.shape[3:])


def stick_breaking_block(qp, q, k, v, k_pos):
    z = jnp.einsum('bqhd,bkhd->bhqk', q.astype(jnp.float32), k.astype(jnp.float32)) * (HEAD_DIM ** -0.5)
    mask = k_pos[None, :] < qp[:, None]
    log_1mb = jnp.where(mask, jax.nn.log_sigmoid(-z), 0.0)
    between = lax.cumsum(log_1mb, axis=3, reverse=True) - log_1mb
    a = jnp.where(mask, jnp.exp(jax.nn.log_sigmoid(z) + between), 0.0)
    return jnp.einsum('bhqk,bkhd->bqhd', a, v.astype(jnp.float32)).astype(v.dtype)


def diff_attn_block(qp, q, k, v, k_pos, lam, slopes):
    s = jnp.einsum('bqhmd,bkhmd->bhmqk', q.astype(jnp.float32), k.astype(jnp.float32)) * (DF_DH ** -0.5)
    dist = (qp[:, None] - k_pos[None, :]).astype(jnp.float32)
    s = s - slopes[None, :, None, None, None] * dist[None, None, None]
    mask = k_pos[None, :] <= qp[:, None]
    p = jax.nn.softmax(jnp.where(mask, s, -jnp.inf), axis=-1)
    a = p[:, :, 0] - lam * p[:, :, 1]
    return jnp.einsum('bhqk,bkhe->bqhe', a, v.astype(jnp.float32)).astype(v.dtype)


def token_mixers(h, q_pos, k_pos, past, w_in, w_o, sb_g, df_g, lam, lam_init, slopes):
    b, t, _ = h.shape
    proj = h @ w_in
    sb_q, sb_k, sb_v, df_q, df_k, df_v = jnp.split(proj, SPLITS, axis=-1)
    sb_q = sb_q.reshape(b, t, SB_HEADS, HEAD_DIM)
    sb_k = sb_k.reshape(b, t, SB_HEADS, HEAD_DIM)
    sb_v = sb_v.reshape(b, t, SB_HEADS, HEAD_DIM)
    df_q = df_q.reshape(b, t, DF_HEADS, 2, DF_DH)
    df_k = df_k.reshape(b, t, DF_HEADS, 2, DF_DH)
    df_v = df_v.reshape(b, t, DF_HEADS, DF_VD)
    new_rows = (sb_k, sb_v, df_k, df_v)
    if past is None:
        ka_sb_k, ka_sb_v, ka_df_k, ka_df_v = new_rows
    else:
        ka_sb_k, ka_sb_v, ka_df_k, ka_df_v = [jnp.concatenate([p_, n_], axis=1) for p_, n_ in zip(past, new_rows)]
    sb_o = sweep_query_blocks(lambda qp, q: stick_breaking_block(qp, q, ka_sb_k, ka_sb_v, k_pos), q_pos, sb_q)
    df_o = sweep_query_blocks(lambda qp, q: diff_attn_block(qp, q, ka_df_k, ka_df_v, k_pos, lam, slopes), q_pos, df_q)
    sb_o = rms_norm(sb_o, sb_g)
    df_o = rms_norm(df_o, df_g) * (1.0 - lam_init)
    mixed = jnp.concatenate([sb_o.reshape(b, t, SB_WIDTH), df_o.reshape(b, t, DF_WIDTH)], axis=-1)
    return mixed @ w_o, new_rows


def conv_ffn(h, prev, w_up, conv_w, conv_b, w_down):
    u = h @ w_up
    t = u.shape[1]
    up = jnp.concatenate([prev, u], axis=1)
    c = conv_b + conv_w[CONV_W - 1] * up[:, CONV_W - 1:CONV_W - 1 + t]
    for i in range(CONV_W - 1):
        c = c + conv_w[i] * up[:, i:i + t]
    g, val = jnp.split(c, 2, axis=-1)
    y = (jax.nn.gelu(g) * val) @ w_down
    return y, up[:, -(CONV_W - 1):]


def setup_inputs(seed: int = 0) -> dict:
    key = jax.random.key(seed)
    ks = jax.random.split(key, 32)
    f32 = jnp.float32
    n_pages = PAST_LEN // PAGE_SIZE
    n_used = DEC_BATCH * n_pages
    n_pool = n_used + n_used // 4
    nrm = lambda k, shp, s=1.0: jax.random.normal(k, shp, f32) * s
    page_table = jax.random.permutation(ks[0], n_pool)[:n_used].reshape(DEC_BATCH, n_pages).astype(jnp.int32)
    return {
        "x_prompt": nrm(ks[1], (BATCH, SEQ, D_MODEL)),
        "x_sample": nrm(ks[2], (DEC_BATCH, DEC_SEQ, D_MODEL)),
        "cache_sb_k": nrm(ks[3], (DEPTH, n_pool, PAGE_SIZE, SB_HEADS, HEAD_DIM)),
        "cache_sb_v": nrm(ks[4], (DEPTH, n_pool, PAGE_SIZE, SB_HEADS, HEAD_DIM)),
        "cache_df_k": nrm(ks[5], (DEPTH, n_pool, PAGE_SIZE, DF_HEADS, 2, DF_DH)),
        "cache_df_v": nrm(ks[6], (DEPTH, n_pool, PAGE_SIZE, DF_HEADS, DF_VD)),
        "state_conv": nrm(ks[7], (DEPTH, DEC_BATCH, CONV_W - 1, 2 * D_FF)),
        "page_table": page_table,
        "norm_mix": 1.0 + nrm(ks[8], (DEPTH, D_MODEL), 0.02),
        "w_in": nrm(ks[9], (DEPTH, D_MODEL, PROJ_WIDTH), D_MODEL ** -0.5),
        "sb_head_norm": 1.0 + nrm(ks[10], (DEPTH, HEAD_DIM), 0.02),
        "df_head_norm": 1.0 + nrm(ks[11], (DEPTH, DF_VD), 0.02),
        "lam_q1": nrm(ks[12], (DEPTH, DF_DH), 0.1),
        "lam_k1": nrm(ks[13], (DEPTH, DF_DH), 0.1),
        "lam_q2": nrm(ks[14], (DEPTH, DF_DH), 0.1),
        "lam_k2": nrm(ks[15], (DEPTH, DF_DH), 0.1),
        "w_o": nrm(ks[16], (DEPTH, MIX_WIDTH, D_MODEL), MIX_WIDTH ** -0.5),
        "norm_ffn": 1.0 + nrm(ks[17], (DEPTH, D_MODEL), 0.02),
        "w_up": nrm(ks[18], (DEPTH, D_MODEL, 2 * D_FF), D_MODEL ** -0.5),
        "conv_w": nrm(ks[19], (DEPTH, CONV_W, 2 * D_FF), CONV_W ** -0.5),
        "conv_b": nrm(ks[20], (DEPTH, 2 * D_FF), 0.02),
        "w_down": nrm(ks[21], (DEPTH, D_FF, D_MODEL), D_FF ** -0.5),
        "norm_final": 1.0 + nrm(ks[22], (D_MODEL,), 0.02),
    }


def reference(x_prompt, x_sample, cache_sb_k, cache_sb_v, cache_df_k, cache_df_v, state_conv, page_table,
              norm_mix, w_in, sb_head_norm, df_head_norm, lam_q1, lam_k1, lam_q2, lam_k2, w_o,
              norm_ffn, w_up, conv_w, conv_b, w_down, norm_final):
    page = cache_sb_k.shape[2]
    past_len = page_table.shape[1] * page
    t_p, t_s = x_prompt.shape[1], x_sample.shape[1]
    pos_p = jnp.arange(t_p, dtype=jnp.int32)
    pos_s = past_len + jnp.arange(t_s, dtype=jnp.int32)
    kpos_s = jnp.arange(past_len + t_s, dtype=jnp.int32)
    slopes = alibi_slopes(DF_HEADS)

    def gather(cache):
        g = cache[page_table]
        return g.reshape(g.shape[0], past_len, *g.shape[3:])

    hp, hs = x_prompt, x_sample
    p_sb_k, p_sb_v, p_df_k, p_df_v, p_conv = [], [], [], [], []
    s_sb_k, s_sb_v, s_df_k, s_df_v, s_conv = [], [], [], [], []
    for l in range(DEPTH):
        lam_init = 0.8 - 0.6 * math.exp(-0.3 * l)
        lam = (jnp.exp(jnp.sum(lam_q1[l].astype(jnp.float32) * lam_k1[l].astype(jnp.float32)))
               - jnp.exp(jnp.sum(lam_q2[l].astype(jnp.float32) * lam_k2[l].astype(jnp.float32))) + lam_init)
        past = (gather(cache_sb_k[l]), gather(cache_sb_v[l]), gather(cache_df_k[l]), gather(cache_df_v[l]))
        mix_p, rows_p = token_mixers(rms_norm(hp, norm_mix[l]), pos_p, pos_p, None, w_in[l], w_o[l],
                                     sb_head_norm[l], df_head_norm[l], lam, lam_init, slopes)
        hp = hp + mix_p
        zeros_prev = jnp.zeros((hp.shape[0], CONV_W - 1, 2 * D_FF), hp.dtype)
        ff_p, conv_p = conv_ffn(rms_norm(hp, norm_ffn[l]), zeros_prev, w_up[l], conv_w[l], conv_b[l], w_down[l])
        hp = hp + ff_p
        mix_s, rows_s = token_mixers(rms_norm(hs, norm_mix[l]), pos_s, kpos_s, past, w_in[l], w_o[l],
                                     sb_head_norm[l], df_head_norm[l], lam, lam_init, slopes)
        hs = hs + mix_s
        ff_s, conv_s = conv_ffn(rms_norm(hs, norm_ffn[l]), state_conv[l], w_up[l], conv_w[l], conv_b[l], w_down[l])
        hs = hs + ff_s
        p_sb_k.append(rows_p[0]); p_sb_v.append(rows_p[1]); p_df_k.append(rows_p[2]); p_df_v.append(rows_p[3])
        p_conv.append(conv_p)
        s_sb_k.append(rows_s[0]); s_sb_v.append(rows_s[1]); s_df_k.append(rows_s[2]); s_df_v.append(rows_s[3])
        s_conv.append(conv_s)
    y_prompt = rms_norm(hp, norm_final)
    y_sample = rms_norm(hs, norm_final)
    return (y_prompt, y_sample,
            jnp.stack(p_sb_k), jnp.stack(p_sb_v), jnp.stack(p_df_k), jnp.stack(p_df_v), jnp.stack(p_conv),
            jnp.stack(s_sb_k), jnp.stack(s_sb_v), jnp.stack(s_df_k), jnp.stack(s_df_v), jnp.stack(s_conv))
```

```python
import functools
import math

import numpy as np
import jax
import jax.numpy as jnp
from jax import lax
from jax.experimental import pallas as pl
from jax.experimental.pallas import tpu as pltpu

F32 = jnp.float32
BF16 = jnp.bfloat16
I32 = jnp.int32

EPS = 1e-6
HEAD_DIM = 64
LANES = 128
SEG = 512
N_SEG = 6
CONV_W = 3
NEG_INF = float("-inf")
VMEM_LIMIT = 48 * 1024 * 1024


def _rms(x, g):
    ms = jnp.mean(x * x, axis=-1, keepdims=True)
    return x * lax.rsqrt(ms + EPS) * g


def _nt_dot(a, b):
    return lax.dot_general(a, b, (((1,), (1,)), ((), ())), preferred_element_type=F32)


def _dot(a, b):
    return jnp.dot(a, b, preferred_element_type=F32)


def _cumsum_matrix(tk):
    j = lax.broadcasted_iota(I32, (2 * tk, tk), 0)
    s = lax.broadcasted_iota(I32, (2 * tk, tk), 1)
    j = jnp.where(j >= tk, j - tk, j)
    return jnp.where(j >= s, 1.0, 0.0).astype(BF16)


def _rev_cumsum(l, uu):
    hi = l.astype(BF16)
    lo = (l - hi.astype(F32)).astype(BF16)
    return _dot(jnp.concatenate([hi, lo], axis=1), uu)


def _log1m_beta(z):
    return -(jnp.maximum(z, 0.0) + jnp.log(1.0 + jnp.exp(-jnp.abs(z))))


def _proj_kernel(x_ref, g_ref, w_ref, sbk_ref, sbv_ref, dfk_ref, dfv_ref,
                 qsb_ref, ksb_ref, vsb_ref, qdf_ref, kdf_ref, vdf_ref):
    h = _rms(x_ref[...], g_ref[...]).astype(BF16)

    def seg(j):
        return _dot(h, w_ref[:, j * SEG:(j + 1) * SEG])

    scale = HEAD_DIM ** -0.5
    qsb_ref[...] = (seg(0) * scale).astype(qsb_ref.dtype)
    k = seg(1)
    sbk_ref[...] = k
    ksb_ref[...] = k.astype(BF16)
    v = seg(2)
    sbv_ref[...] = v
    vsb_ref[...] = v.astype(BF16)
    qdf_ref[...] = (seg(3) * scale).astype(qdf_ref.dtype)
    k = seg(4)
    dfk_ref[...] = k
    kdf_ref[...] = k.astype(BF16)
    v = seg(5)
    dfv_ref[...] = v
    vdf_ref[...] = v.astype(BF16)


def _proj(x, g, w, q_dtype, tm=256):
    m, d = x.shape
    row = lambda i: (i, 0)
    const = lambda i: (0, 0)
    f32o = jax.ShapeDtypeStruct((m, SEG), F32)
    b16o = jax.ShapeDtypeStruct((m, SEG), BF16)
    qo = jax.ShapeDtypeStruct((m, SEG), q_dtype)
    ospec = pl.BlockSpec((tm, SEG), row)
    return pl.pallas_call(
        _proj_kernel,
        grid=(m // tm,),
        in_specs=[pl.BlockSpec((tm, d), row), pl.BlockSpec((1, d), const),
                  pl.BlockSpec((d, N_SEG * SEG), const)],
        out_specs=[ospec] * 10,
        out_shape=[f32o, f32o, f32o, f32o, qo, b16o, b16o, qo, b16o, b16o],
        compiler_params=pltpu.CompilerParams(dimension_semantics=("arbitrary",),
                                             vmem_limit_bytes=VMEM_LIMIT),
        name="proj",
    )(x, g, w)


def _split_lane_halves_rows(x):
    lo = lax.broadcasted_iota(I32, x.shape, 1) < HEAD_DIM
    zero = jnp.zeros_like(x)
    return jnp.concatenate([jnp.where(lo, x, zero), jnp.where(lo, zero, x)], axis=0)


def _sb_prompt_kernel(q_ref, k_ref, v_ref, g_ref, o_ref, *, tq):
    tk = tq
    qi = pl.program_id(2)
    q2 = _split_lane_halves_rows(q_ref[...])
    uu = _cumsum_matrix(tk)
    row = lax.broadcasted_iota(I32, (2 * tq, tk), 0)
    col = lax.broadcasted_iota(I32, (2 * tq, tk), 1)
    row = jnp.where(row >= tq, row - tq, row)
    strict = col < row

    def block(kj, carry, masked):
        c, acc = carry
        start = pl.multiple_of(kj * tk, tk)
        kb = k_ref[pl.ds(start, tk), :]
        vb = v_ref[pl.ds(start, tk), :]
        z = _nt_dot(q2, kb)
        l = _log1m_beta(z)
        if masked:
            l = jnp.where(strict, l, 0.0)
        cinc = _rev_cumsum(l, uu)
        a = jnp.exp(z + c + cinc)
        if masked:
            a = jnp.where(strict, a, 0.0)
        c = c + cinc[:, 0:1]
        ab = a.astype(BF16)
        a_cat = jnp.concatenate([ab[:tq], ab[tq:]], axis=1)
        acc = acc + _dot(a_cat, _split_lane_halves_rows(vb))
        return c, acc

    carry = (jnp.zeros((2 * tq, 1), F32), jnp.zeros((tq, LANES), F32))
    carry = block(qi, carry, True)
    _, o = lax.fori_loop(0, qi, lambda i, cr: block(qi - 1 - i, cr, False), carry)

    lo = lax.broadcasted_iota(I32, o.shape, 1) < HEAD_DIM
    o2 = o * o
    s_lo = jnp.sum(jnp.where(lo, o2, 0.0), axis=-1, keepdims=True)
    s_hi = jnp.sum(jnp.where(lo, 0.0, o2), axis=-1, keepdims=True)
    ms = jnp.where(lo, s_lo, s_hi) * (1.0 / HEAD_DIM)
    o_ref[...] = (o * lax.rsqrt(ms + EPS) * g_ref[...]).astype(o_ref.dtype)


def _lam_value(lam_ref, lam_init):
    lv = lam_ref[...]
    s1 = jnp.sum(lv[0:1] * lv[1:2], axis=-1, keepdims=True)
    s2 = jnp.sum(lv[2:3] * lv[3:4], axis=-1, keepdims=True)
    return jnp.exp(s1) - jnp.exp(s2) + lam_init


def _df_prompt_kernel(slopes_ref, lam_ref, q_ref, k_ref, v_ref, g_ref, o_ref, *, tq, lam_init):
    tk = tq
    qi = pl.program_id(2)
    slope = slopes_ref[pl.program_id(1)]
    q2 = _split_lane_halves_rows(q_ref[...])
    row = lax.broadcasted_iota(I32, (2 * tq, tk), 0)
    col = lax.broadcasted_iota(I32, (2 * tq, tk), 1)
    row = jnp.where(row >= tq, row - tq, row)
    causal = col <= row
    bias0 = slope * (row - col).astype(F32)

    def block(kj, carry, masked):
        m_i, l_i, acc = carry
        start = pl.multiple_of(kj * tk, tk)
        kb = k_ref[pl.ds(start, tk), :]
        vb = v_ref[pl.ds(start, tk), :]
        off = slope * ((qi - kj) * tq).astype(F32)
        s = _nt_dot(q2, kb) - bias0 - off
        if masked:
            s = jnp.where(causal, s, NEG_INF)
        m_new = jnp.maximum(m_i, jnp.max(s, axis=-1, keepdims=True))
        alpha = jnp.exp(m_i - m_new)
        p = jnp.exp(s - m_new)
        l_i = alpha * l_i + jnp.sum(p, axis=-1, keepdims=True)
        acc = alpha * acc + _dot(p.astype(BF16), vb)
        return m_new, l_i, acc

    carry = (jnp.full((2 * tq, 1), NEG_INF, F32), jnp.zeros((2 * tq, 1), F32),
             jnp.zeros((2 * tq, LANES), F32))
    carry = block(qi, carry, True)
    _, l_i, acc = lax.fori_loop(0, qi, lambda i, cr: block(qi - 1 - i, cr, False), carry)

    lam = _lam_value(lam_ref, lam_init)
    o = acc[:tq] / l_i[:tq] - lam * (acc[tq:] / l_i[tq:])
    o_ref[...] = (_rms(o, g_ref[...]) * (1.0 - lam_init)).astype(o_ref.dtype)


def _prompt_attention(qsb, ksb, vsb, qdf, kdf, vdf, g_sb, g_df, slopes, lam_vecs, lam_init, tq=128):
    b, t, _ = qsb.shape
    n_blk = SEG // LANES
    grid = (b, n_blk, t // tq)
    qspec = pl.BlockSpec((None, tq, LANES), lambda bi, p, qi: (bi, qi, p))
    kvspec = pl.BlockSpec((None, t, LANES), lambda bi, p, qi: (bi, 0, p))
    gspec = pl.BlockSpec((1, LANES), lambda bi, p, qi: (0, 0))
    out = jax.ShapeDtypeStruct((b, t, SEG), BF16)
    params = pltpu.CompilerParams(dimension_semantics=("arbitrary",) * 3, vmem_limit_bytes=VMEM_LIMIT)
    o_sb = pl.pallas_call(
        functools.partial(_sb_prompt_kernel, tq=tq),
        grid=grid, in_specs=[qspec, kvspec, kvspec, gspec], out_specs=qspec, out_shape=out,
        compiler_params=params, name="sb_prompt",
    )(qsb, ksb, vsb, g_sb)
    o_df = pl.pallas_call(
        functools.partial(_df_prompt_kernel, tq=tq, lam_init=lam_init),
        grid=grid,
        in_specs=[pl.BlockSpec(memory_space=pltpu.SMEM),
                  pl.BlockSpec((4, HEAD_DIM), lambda bi, p, qi: (0, 0)),
                  qspec, kvspec, kvspec, gspec],
        out_specs=qspec, out_shape=out, compiler_params=params, name="df_prompt",
    )(slopes, lam_vecs, qdf, kdf, vdf, g_df)
    return o_sb, o_df


def _decode_kernel(pt_ref, slopes_ref, lam_ref, qsb_ref, qdf_ref, nsbk_ref, nsbv_ref, ndfk_ref, ndfv_ref,
                   gsb_ref, gdf_ref, *rest, n_pages, pages_per_step, page, t_new, lam_init):
    pps = pages_per_step
    csbk = rest[0 * pps:1 * pps]
    csbv = rest[1 * pps:2 * pps]
    cdfk = rest[2 * pps:3 * pps]
    cdfv = rest[3 * pps:4 * pps]
    osb_ref, odf_ref = rest[4 * pps:4 * pps + 2]
    qrsb_ref, qrdf_ref, c_ref, m_ref, l_ref, asb_ref, adf_ref = rest[4 * pps + 2:]

    j = pl.program_id(1)
    n_rows = SEG // HEAD_DIM * t_new
    past_len = n_pages * page
    rowq = lax.broadcasted_iota(I32, (n_rows, 1), 0)
    i_q = rowq % t_new
    slope = jnp.zeros((n_rows, 1), F32)
    for h in range(SEG // LANES):
        slope = jnp.where(rowq // (2 * t_new) == h, slopes_ref[h], slope)
    col = lax.broadcasted_iota(I32, (n_rows, page), 1)
    uu = _cumsum_matrix(page)

    def sb_weights(z, mask):
        l = _log1m_beta(z)
        if mask is not None:
            l = jnp.where(mask, l, 0.0)
        cinc = _rev_cumsum(l, uu)
        c = c_ref[...]
        a = jnp.exp(z + c + cinc)
        if mask is not None:
            a = jnp.where(mask, a, 0.0)
        c_ref[...] = c + cinc[:, 0:1]
        return a.astype(BF16)

    def df_weights(s, mask):
        if mask is not None:
            s = jnp.where(mask, s, NEG_INF)
        m_i = m_ref[...]
        m_new = jnp.maximum(m_i, jnp.max(s, axis=-1, keepdims=True))
        alpha = jnp.exp(m_i - m_new)
        p = jnp.exp(s - m_new)
        l_ref[...] = alpha * l_ref[...] + jnp.sum(p, axis=-1, keepdims=True)
        m_ref[...] = m_new
        return p.astype(BF16), alpha

    def df_accumulate(p, alpha, v_of_head):
        rows_per_head = 2 * t_new
        for h in range(SEG // LANES):
            rows = slice(h * rows_per_head, (h + 1) * rows_per_head)
            adf_ref[rows, :] = alpha[rows] * adf_ref[rows, :] + _dot(p[rows], v_of_head(h))

    def pad_rows(x):
        return jnp.concatenate([x, jnp.zeros((page - t_new, SEG), F32)], axis=0).astype(BF16)

    @pl.when(j == 0)
    def _():
        lane_chunk = lax.broadcasted_iota(I32, (n_rows, SEG), 1) // HEAD_DIM
        own = lane_chunk == lax.broadcasted_iota(I32, (n_rows, SEG), 0) // t_new
        reps = n_rows // t_new
        qrsb_ref[...] = jnp.where(own, jnp.concatenate([qsb_ref[...]] * reps, axis=0), 0.0).astype(BF16)
        qrdf_ref[...] = jnp.where(own, jnp.concatenate([qdf_ref[...]] * reps, axis=0), 0.0).astype(BF16)
        c_ref[...] = jnp.zeros_like(c_ref)
        m_ref[...] = jnp.full_like(m_ref, NEG_INF)
        l_ref[...] = jnp.zeros_like(l_ref)
        asb_ref[...] = jnp.zeros_like(asb_ref)
        adf_ref[...] = jnp.zeros_like(adf_ref)
        a = sb_weights(_nt_dot(qrsb_ref[...], pad_rows(nsbk_ref[...])), col < i_q)
        asb_ref[...] += _dot(a, pad_rows(nsbv_ref[...]))
        s = _nt_dot(qrdf_ref[...], pad_rows(ndfk_ref[...])) - slope * (i_q - col).astype(F32)
        p, alpha = df_weights(s, col <= i_q)
        vnew = pad_rows(ndfv_ref[...])
        df_accumulate(p, alpha, lambda h: vnew[:, h * LANES:(h + 1) * LANES])

    for u in range(pps):
        lp = n_pages - 1 - (j * pps + u)
        dist = (past_len + i_q - lp * page - col).astype(F32)
        a = sb_weights(_dot(qrsb_ref[...], csbk[u][...].astype(BF16)), None)
        asb_ref[...] += _nt_dot(a, csbv[u][...].astype(BF16))
        s = _dot(qrdf_ref[...], cdfk[u][...].astype(BF16)) - slope * dist
        p, alpha = df_weights(s, None)
        df_accumulate(p, alpha, lambda h: cdfv[u][:, h, :].astype(BF16))

    @pl.when(j == pl.num_programs(1) - 1)
    def _():
        lane = lax.broadcasted_iota(I32, (t_new, SEG), 1)
        acc = asb_ref[...]
        o = jnp.zeros((t_new, SEG), F32)
        for h in range(SEG // HEAD_DIM):
            o = o + jnp.where(lane // HEAD_DIM == h, acc[h * t_new:(h + 1) * t_new], 0.0)
        o2 = o * o
        ms = jnp.zeros((t_new, SEG), F32)
        for h in range(SEG // HEAD_DIM):
            sel = lane // HEAD_DIM == h
            ms = jnp.where(sel, jnp.sum(jnp.where(sel, o2, 0.0), axis=-1, keepdims=True), ms)
        osb_ref[...] = o * lax.rsqrt(ms * (1.0 / HEAD_DIM) + EPS) * gsb_ref[...]

        lam = _lam_value(lam_ref, lam_init)
        acc = adf_ref[...]
        l_i = l_ref[...]
        outs = []
        for h in range(SEG // LANES):
            r0 = 2 * h * t_new
            o0 = acc[r0:r0 + t_new] / l_i[r0:r0 + t_new]
            o1 = acc[r0 + t_new:r0 + 2 * t_new] / l_i[r0 + t_new:r0 + 2 * t_new]
            outs.append(_rms(o0 - lam * o1, gdf_ref[...]) * (1.0 - lam_init))
        odf_ref[...] = jnp.concatenate(outs, axis=1)


def _decode_attention(page_table, slopes, lam_vecs, qsb, qdf, nsbk, nsbv, ndfk, ndfv, g_sb, g_df,
                      c_sbk, c_sbv, c_dfk, c_dfv, lam_init, pages_per_step=8):
    nb, t_new, _ = qsb.shape
    n_pages = page_table.shape[1]
    page = c_sbk.shape[2]
    pps = pages_per_step
    n_rows = SEG // HEAD_DIM * t_new
    per_b = pl.BlockSpec((None, t_new, SEG), lambda b, j, pt: (b, 0, 0))
    const2 = lambda shape: pl.BlockSpec(shape, lambda b, j, pt: (0, 0))

    def page_spec(u, block):
        zeros = (0,) * (len(block) - 1)
        return pl.BlockSpec(block, lambda b, j, pt: (pt[b, n_pages - 1 - (j * pps + u)],) + zeros)

    t_specs = [page_spec(u, (None, SEG, page)) for u in range(pps)]
    v_specs = [page_spec(u, (None, page, SEG // LANES, LANES)) for u in range(pps)]
    in_specs = ([pl.BlockSpec(memory_space=pltpu.SMEM), const2((4, HEAD_DIM))] + [per_b] * 6
                + [const2((1, SEG)), const2((1, LANES))] + t_specs * 3 + v_specs)
    grid_spec = pltpu.PrefetchScalarGridSpec(
        num_scalar_prefetch=1,
        grid=(nb, n_pages // pps),
        in_specs=in_specs,
        out_specs=[per_b, per_b],
        scratch_shapes=[pltpu.VMEM((n_rows, SEG), BF16), pltpu.VMEM((n_rows, SEG), BF16),
                        pltpu.VMEM((n_rows, 1), F32), pltpu.VMEM((n_rows, 1), F32),
                        pltpu.VMEM((n_rows, 1), F32),
                        pltpu.VMEM((n_rows, SEG), F32), pltpu.VMEM((n_rows, LANES), F32)],
    )
    o = jax.ShapeDtypeStruct((nb, t_new, SEG), F32)
    return pl.pallas_call(
        functools.partial(_decode_kernel, n_pages=n_pages, pages_per_step=pps, page=page,
                          t_new=t_new, lam_init=lam_init),
        grid_spec=grid_spec, out_shape=[o, o],
        compiler_params=pltpu.CompilerParams(dimension_semantics=("arbitrary", "arbitrary"),
                                             vmem_limit_bytes=VMEM_LIMIT),
        name="decode_attn",
    )(page_table, slopes, lam_vecs, qsb, qdf, nsbk, nsbv, ndfk, ndfv, g_sb, g_df,
      *([c_sbk] * pps + [c_sbv] * pps + [c_dfk] * pps + [c_dfv] * pps))


def _gelu_tanh(x):
    return 0.5 * x * (1.0 + jnp.tanh(math.sqrt(2.0 / math.pi) * (x + 0.044715 * x * x * x)))


def _post_kernel(x_ref, msb_ref, mdf_ref, woa_ref, wob_ref, gffn_ref, wup_ref, cw_ref, cb_ref, wdn_ref,
                 gfin_ref, *rest, d_ff, fc, seq_rows):
    if seq_rows is None:
        y_ref, conv_ref, carry_ref = rest
    else:
        p1_ref, p2_ref, y_ref, conv_ref = rest
    tm = x_ref.shape[0]
    h1 = x_ref[...] + _dot(msb_ref[...].astype(BF16), woa_ref[...]) + _dot(mdf_ref[...].astype(BF16), wob_ref[...])
    h2 = _rms(h1, gffn_ref[...]).astype(BF16)
    rowi = lax.broadcasted_iota(I32, (tm, fc), 0)

    if seq_rows is None:
        @pl.when(pl.program_id(1) == 0)
        def _():
            carry_ref[...] = jnp.zeros_like(carry_ref)

    def conv(cols):
        u = _dot(h2, wup_ref[:, cols])
        r1 = pltpu.roll(u, 1, axis=0)
        r2 = pltpu.roll(u, 2, axis=0)
        if seq_rows is None:
            prev = carry_ref[:, cols]
            r1 = jnp.where(rowi == 0, prev[7:8], r1)
            r2 = jnp.where(rowi == 0, prev[6:7], jnp.where(rowi == 1, prev[7:8], r2))
            carry_ref[:, cols] = u[tm - 8:]
            conv_ref[:, cols] = u[tm - 8:]
        else:
            pos = rowi % seq_rows
            r1 = jnp.where(pos == 0, p1_ref[:, cols], r1)
            r2 = jnp.where(pos < 2, p2_ref[:, cols], r2)
            conv_ref[:, cols] = u
        cw = cw_ref[:, cols]
        return cb_ref[:, cols] + cw[2:3] * u + cw[1:2] * r1 + cw[0:1] * r2

    acc = jnp.zeros((tm, x_ref.shape[1]), F32)
    for c in range(d_ff // fc):
        gate = conv(slice(c * fc, (c + 1) * fc))
        val = conv(slice(d_ff + c * fc, d_ff + (c + 1) * fc))
        act = (_gelu_tanh(gate) * val).astype(BF16)
        acc = acc + _dot(act, wdn_ref[c * fc:(c + 1) * fc, :])
    y_ref[...] = _rms(h1 + acc, gfin_ref[...])


def _post(x, msb, mdf, woa, wob, gffn, wup, cw, cb, wdn, gfin, prev=None, tm=256, fc=256):
    d = x.shape[-1]
    d_ff = wdn.shape[0]
    weights = (woa, wob, gffn, wup, cw, cb, wdn, gfin)
    if prev is None:
        b, t, _ = x.shape
        grid = (b, t // tm)
        act = lambda w: pl.BlockSpec((None, tm, w), lambda bi, ti: (bi, ti, 0))
        const = lambda a: pl.BlockSpec(a.shape, lambda bi, ti: (0,) * a.ndim)
        in_specs = [act(d), act(SEG), act(SEG)] + [const(a) for a in weights]
        out_specs = [act(d), pl.BlockSpec((None, 8, 2 * d_ff), lambda bi, ti: (bi, 0, 0))]
        out_shape = [jax.ShapeDtypeStruct((b, t, d), F32), jax.ShapeDtypeStruct((b, 8, 2 * d_ff), F32)]
        scratch = [pltpu.VMEM((8, 2 * d_ff), F32)]
        args = (x, msb, mdf) + weights
        kern = functools.partial(_post_kernel, d_ff=d_ff, fc=fc, seq_rows=None)
        name = "post_prompt"
    else:
        p1, p2, seq_rows = prev
        m = x.shape[0]
        grid = (m // tm, 1)
        act = lambda w: pl.BlockSpec((tm, w), lambda mi, ti: (mi, 0))
        const = lambda a: pl.BlockSpec(a.shape, lambda mi, ti: (0,) * a.ndim)
        in_specs = ([act(d), act(SEG), act(SEG)] + [const(a) for a in weights]
                    + [act(2 * d_ff), act(2 * d_ff)])
        out_specs = [act(d), act(2 * d_ff)]
        out_shape = [jax.ShapeDtypeStruct((m, d), F32), jax.ShapeDtypeStruct((m, 2 * d_ff), F32)]
        scratch = []
        args = (x, msb, mdf) + weights + (p1, p2)
        kern = functools.partial(_post_kernel, d_ff=d_ff, fc=fc, seq_rows=seq_rows)
        name = "post_sample"
    return pl.pallas_call(
        kern, grid=grid, in_specs=in_specs, out_specs=out_specs, out_shape=out_shape,
        scratch_shapes=scratch,
        compiler_params=pltpu.CompilerParams(dimension_semantics=("arbitrary", "arbitrary"),
                                             vmem_limit_bytes=56 * 1024 * 1024),
        name=name,
    )(*args)


def kernel(x_prompt, x_sample, cache_sb_k, cache_sb_v, cache_df_k, cache_df_v, state_conv, page_table,
           norm_mix, w_in, sb_head_norm, df_head_norm, lam_q1, lam_k1, lam_q2, lam_k2, w_o,
           norm_ffn, w_up, conv_w, conv_b, w_down, norm_final):
    depth = w_in.shape[0]
    assert depth == 1, "single-layer step"
    b, t, d = x_prompt.shape
    nb, t_new, _ = x_sample.shape
    n_pool, page = cache_sb_k.shape[1], cache_sb_k.shape[2]
    d_ff = w_down.shape[1]
    sb_heads = cache_sb_k.shape[3]
    df_heads = cache_df_k.shape[3]
    lam_init = 0.8 - 0.6 * math.exp(-0.3 * 0)
    slopes = jnp.asarray(2.0 ** (-8.0 * np.arange(1, df_heads + 1) / df_heads), dtype=F32)

    g_mix = norm_mix[0].reshape(1, d)
    w_in_b = w_in[0].astype(BF16)
    g_sb = jnp.tile(sb_head_norm[0], LANES // HEAD_DIM).reshape(1, LANES)
    g_sb_full = jnp.tile(sb_head_norm[0], SEG // HEAD_DIM).reshape(1, SEG)
    g_df = df_head_norm[0].reshape(1, LANES)
    lam_vecs = jnp.stack([lam_q1[0], lam_k1[0], lam_q2[0], lam_k2[0]]).astype(F32)
    wo_b = w_o[0].astype(BF16)
    post_w = (wo_b[:SEG], wo_b[SEG:], norm_ffn[0].reshape(1, d), w_up[0].astype(BF16), conv_w[0],
              conv_b[0].reshape(1, 2 * d_ff), w_down[0].astype(BF16), norm_final.reshape(1, d))

    (p_sbk, p_sbv, p_dfk, p_dfv, qsb, ksb, vsb, qdf, kdf, vdf) = _proj(
        x_prompt.reshape(b * t, d), g_mix, w_in_b, BF16)
    r3 = lambda a: a.reshape(b, t, SEG)
    o_sb, o_df = _prompt_attention(r3(qsb), r3(ksb), r3(vsb), r3(qdf), r3(kdf), r3(vdf),
                                   g_sb, g_df, slopes, lam_vecs, lam_init)
    y_prompt, conv_tail = _post(x_prompt, o_sb, o_df, *post_w)
    p_conv = conv_tail[:, 8 - (CONV_W - 1):]

    (s_sbk, s_sbv, s_dfk, s_dfv, sqsb, _, _, sqdf, _, _) = _proj(
        x_sample.reshape(nb * t_new, d), g_mix, w_in_b, F32)
    s3 = lambda a: a.reshape(nb, t_new, SEG)
    pos_minor = lambda c: jnp.moveaxis(c.reshape(n_pool, page, SEG), 1, 2)
    so_sb, so_df = _decode_attention(
        page_table, slopes, lam_vecs, s3(sqsb), s3(sqdf), s3(s_sbk), s3(s_sbv), s3(s_dfk), s3(s_dfv),
        g_sb_full, g_df, pos_minor(cache_sb_k), pos_minor(cache_sb_v), pos_minor(cache_df_k),
        cache_df_v.reshape(n_pool, page, df_heads, 2 * HEAD_DIM), lam_init)
    st = state_conv[0]
    zeros = jnp.zeros((nb, t_new - 1, 2 * d_ff), F32)
    prev1 = jnp.concatenate([st[:, 1:2], zeros], axis=1).reshape(nb * t_new, 2 * d_ff)
    prev2 = jnp.concatenate([st, zeros[:, 1:]], axis=1).reshape(nb * t_new, 2 * d_ff)
    y_sample, u_s = _post(x_sample.reshape(nb * t_new, d), so_sb.reshape(nb * t_new, SEG),
                          so_df.reshape(nb * t_new, SEG), *post_w, prev=(prev1, prev2, t_new))
    s_conv = u_s.reshape(nb, t_new, 2 * d_ff)[:, t_new - (CONV_W - 1):]

    return (y_prompt, y_sample.reshape(nb, t_new, d),
            p_sbk.reshape(1, b, t, sb_heads, HEAD_DIM), p_sbv.reshape(1, b, t, sb_heads, HEAD_DIM),
            p_dfk.reshape(1, b, t, df_heads, 2, HEAD_DIM), p_dfv.reshape(1, b, t, df_heads, 2 * HEAD_DIM),
            p_conv[None],
            s_sbk.reshape(1, nb, t_new, sb_heads, HEAD_DIM), s_sbv.reshape(1, nb, t_new, sb_heads, HEAD_DIM),
            s_dfk.reshape(1, nb, t_new, df_heads, 2, HEAD_DIM), s_dfv.reshape(1, nb, t_new, df_heads, 2 * HEAD_DIM),
            s_conv[None])
```

```python
import functools
import math

import numpy as np
import jax
import jax.numpy as jnp
from jax import lax
from jax.experimental import pallas as pl
from jax.experimental.pallas import tpu as pltpu

F32 = jnp.float32
BF16 = jnp.bfloat16
I32 = jnp.int32

EPS = 1e-6
HEAD_DIM = 64
LANES = 128
SEG = 512
N_SEG = 6
CONV_W = 3
NEG_INF = float("-inf")
VMEM_LIMIT = 48 * 1024 * 1024


def _rms(x, g):
    ms = jnp.mean(x * x, axis=-1, keepdims=True)
    return x * lax.rsqrt(ms + EPS) * g


def _nt_dot(a, b):
    return lax.dot_general(a, b, (((1,), (1,)), ((), ())), preferred_element_type=F32)


def _dot(a, b):
    return jnp.dot(a, b, preferred_element_type=F32)


def _cumsum_matrix(tk):
    j = lax.broadcasted_iota(I32, (2 * tk, tk), 0)
    s = lax.broadcasted_iota(I32, (2 * tk, tk), 1)
    j = jnp.where(j >= tk, j - tk, j)
    return jnp.where(j >= s, 1.0, 0.0).astype(BF16)


def _rev_cumsum(l, uu):
    hi = l.astype(BF16)
    lo = (l - hi.astype(F32)).astype(BF16)
    return _dot(jnp.concatenate([hi, lo], axis=1), uu)


def _log1m_beta(z):
    return -(jnp.maximum(z, 0.0) + jnp.log(1.0 + jnp.exp(-jnp.abs(z))))


def _proj_kernel(x_ref, g_ref, w_ref, sbk_ref, sbv_ref, dfk_ref, dfv_ref,
                 qsb_ref, ksb_ref, vsb_ref, qdf_ref, kdf_ref, vdf_ref):
    h = _rms(x_ref[...], g_ref[...]).astype(BF16)

    def seg(j):
        return _dot(h, w_ref[:, j * SEG:(j + 1) * SEG])

    scale = HEAD_DIM ** -0.5
    qsb_ref[...] = (seg(0) * scale).astype(qsb_ref.dtype)
    k = seg(1)
    sbk_ref[...] = k
    ksb_ref[...] = k.astype(BF16)
    v = seg(2)
    sbv_ref[...] = v
    vsb_ref[...] = v.astype(BF16)
    qdf_ref[...] = (seg(3) * scale).astype(qdf_ref.dtype)
    k = seg(4)
    dfk_ref[...] = k
    kdf_ref[...] = k.astype(BF16)
    v = seg(5)
    dfv_ref[...] = v
    vdf_ref[...] = v.astype(BF16)


def _proj(x, g, w, q_dtype, tm=256):
    m, d = x.shape
    row = lambda i: (i, 0)
    const = lambda i: (0, 0)
    f32o = jax.ShapeDtypeStruct((m, SEG), F32)
    b16o = jax.ShapeDtypeStruct((m, SEG), BF16)
    qo = jax.ShapeDtypeStruct((m, SEG), q_dtype)
    ospec = pl.BlockSpec((tm, SEG), row)
    return pl.pallas_call(
        _proj_kernel,
        grid=(m // tm,),
        in_specs=[pl.BlockSpec((tm, d), row), pl.BlockSpec((1, d), const),
                  pl.BlockSpec((d, N_SEG * SEG), const)],
        out_specs=[ospec] * 10,
        out_shape=[f32o, f32o, f32o, f32o, qo, b16o, b16o, qo, b16o, b16o],
        compiler_params=pltpu.CompilerParams(dimension_semantics=("arbitrary",),
                                             vmem_limit_bytes=VMEM_LIMIT),
        name="proj",
    )(x, g, w)


def _split_lane_halves_rows(x):
    lo = lax.broadcasted_iota(I32, x.shape, 1) < HEAD_DIM
    zero = jnp.zeros_like(x)
    return jnp.concatenate([jnp.where(lo, x, zero), jnp.where(lo, zero, x)], axis=0)


F32_EXP_ZERO = -104.0


def _sb_prompt_kernel(q_ref, k_ref, v_ref, g_ref, o_ref, q2_ref, c_ref, acc_ref, *, tq, tk):
    qi = pl.program_id(1)
    n_blk = SEG // LANES
    for p in range(n_blk):
        q2_ref[p] = _split_lane_halves_rows(q_ref[:, p * LANES:(p + 1) * LANES])
    c_ref[...] = jnp.zeros_like(c_ref)
    acc_ref[...] = jnp.zeros_like(acc_ref)
    uu = _cumsum_matrix(tk)
    row = lax.broadcasted_iota(I32, (2 * tq, tk), 0)
    col = lax.broadcasted_iota(I32, (2 * tq, tk), 1)
    qpos = qi * tq + jnp.where(row >= tq, row - tq, row)

    def block(kj, masked):
        start = pl.multiple_of(kj * tk, tk)
        if masked:
            strict = start + col < qpos
        for p in range(n_blk):
            lanes = slice(p * LANES, (p + 1) * LANES)
            z = _nt_dot(q2_ref[p], k_ref[pl.ds(start, tk), lanes])
            l = _log1m_beta(z)
            if masked:
                l = jnp.where(strict, l, 0.0)
            cinc = _rev_cumsum(l, uu)
            c = c_ref[p]
            a = jnp.exp(z + c + cinc)
            if masked:
                a = jnp.where(strict, a, 0.0)
            c_ref[p] = c + cinc[:, 0:1]
            ab = a.astype(BF16)
            a_cat = jnp.concatenate([ab[:tq], ab[tq:]], axis=1)
            acc_ref[p] += _dot(a_cat, _split_lane_halves_rows(v_ref[pl.ds(start, tk), lanes]))

    n_diag = tq // tk
    top = (qi + 1) * n_diag - 1
    for d in range(n_diag):
        block(top - d, True)

    def c_max():
        c = c_ref[0]
        for p in range(1, n_blk):
            c = jnp.maximum(c, c_ref[p])
        return jnp.max(c)

    def body(carry):
        kj, _ = carry
        block(kj, False)
        return kj - 1, c_max()

    lax.while_loop(lambda carry: (carry[0] >= 0) & (carry[1] > F32_EXP_ZERO), body,
                   (top - n_diag, c_max()))

    lo = lax.broadcasted_iota(I32, (tq, LANES), 1) < HEAD_DIM
    for p in range(n_blk):
        o = acc_ref[p]
        o2 = o * o
        s_lo = jnp.sum(jnp.where(lo, o2, 0.0), axis=-1, keepdims=True)
        s_hi = jnp.sum(jnp.where(lo, 0.0, o2), axis=-1, keepdims=True)
        ms = jnp.where(lo, s_lo, s_hi) * (1.0 / HEAD_DIM)
        o_ref[:, p * LANES:(p + 1) * LANES] = (o * lax.rsqrt(ms + EPS) * g_ref[...]).astype(o_ref.dtype)


def _lam_value(lam_ref, lam_init):
    lv = lam_ref[...]
    s1 = jnp.sum(lv[0:1] * lv[1:2], axis=-1, keepdims=True)
    s2 = jnp.sum(lv[2:3] * lv[3:4], axis=-1, keepdims=True)
    return jnp.exp(s1) - jnp.exp(s2) + lam_init


def _df_prompt_kernel(slopes_ref, lam_ref, q_ref, k_ref, v_ref, g_ref, o_ref, q2_ref, m_ref, acc_ref,
                      *, tq, tk, lam_init):
    qi = pl.program_id(1)
    n_heads = SEG // LANES
    for h in range(n_heads):
        q2_ref[h] = _split_lane_halves_rows(q_ref[:, h * LANES:(h + 1) * LANES])
    m_ref[...] = jnp.full_like(m_ref, NEG_INF)
    acc_ref[...] = jnp.zeros_like(acc_ref)
    row = lax.broadcasted_iota(I32, (2 * tq, tk), 0)
    col = lax.broadcasted_iota(I32, (2 * tq, tk), 1)
    rel = jnp.where(row >= tq, row - tq, row) - col
    ones = jnp.ones((tk, LANES), BF16)

    def block(kj, masked):
        start = pl.multiple_of(kj * tk, tk)
        dist_i = rel + (qi * tq - start)
        dist = dist_i.astype(F32)
        for h in range(n_heads):
            lanes = slice(h * LANES, (h + 1) * LANES)
            s = _nt_dot(q2_ref[h], k_ref[pl.ds(start, tk), lanes]) - slopes_ref[h] * dist
            if masked:
                s = jnp.where(dist_i >= 0, s, NEG_INF)
            m_i = m_ref[h]
            m_new = jnp.maximum(m_i, jnp.max(s, axis=-1, keepdims=True))
            alpha = jnp.exp(m_i - m_new)
            p = jnp.exp(s - m_new).astype(BF16)
            v_aug = jnp.concatenate([v_ref[pl.ds(start, tk), lanes], ones], axis=1)
            acc_ref[h] = alpha * acc_ref[h] + _dot(p, v_aug)
            m_ref[h] = m_new

    top = ((qi + 1) * tq - 1) // tk
    block(top, True)
    lax.fori_loop(0, top, lambda i, _: block(top - 1 - i, False), None)

    lam = _lam_value(lam_ref, lam_init)
    for h in range(n_heads):
        acc = acc_ref[h]
        r = acc[:, :LANES] / acc[:, LANES:]
        o = r[:tq] - lam * r[tq:]
        o_ref[:, h * LANES:(h + 1) * LANES] = (_rms(o, g_ref[...]) * (1.0 - lam_init)).astype(o_ref.dtype)


def _prompt_attention(qsb, ksb, vsb, qdf, kdf, vdf, g_sb, g_df, slopes, lam_vecs, lam_init,
                      tq=256, tk_sb=128, tk_df=512):
    b, t, _ = qsb.shape
    n_blk = SEG // LANES
    grid = (b, t // tq)
    qspec = pl.BlockSpec((None, tq, SEG), lambda bi, qi: (bi, qi, 0))
    kvspec = pl.BlockSpec((None, t, SEG), lambda bi, qi: (bi, 0, 0))
    gspec = pl.BlockSpec((1, LANES), lambda bi, qi: (0, 0))
    out = jax.ShapeDtypeStruct((b, t, SEG), BF16)
    params = pltpu.CompilerParams(dimension_semantics=("arbitrary",) * 2, vmem_limit_bytes=VMEM_LIMIT)
    q2_scratch = pltpu.VMEM((n_blk, 2 * tq, LANES), BF16)
    o_sb = pl.pallas_call(
        functools.partial(_sb_prompt_kernel, tq=tq, tk=tk_sb),
        grid=grid, in_specs=[qspec, kvspec, kvspec, gspec], out_specs=qspec, out_shape=out,
        scratch_shapes=[q2_scratch, pltpu.VMEM((n_blk, 2 * tq, 1), F32), pltpu.VMEM((n_blk, tq, LANES), F32)],
        compiler_params=params, name="sb_prompt",
    )(qsb, ksb, vsb, g_sb)
    o_df = pl.pallas_call(
        functools.partial(_df_prompt_kernel, tq=tq, tk=tk_df, lam_init=lam_init),
        grid=grid,
        in_specs=[pl.BlockSpec(memory_space=pltpu.SMEM),
                  pl.BlockSpec((4, HEAD_DIM), lambda bi, qi: (0, 0)),
                  qspec, kvspec, kvspec, gspec],
        out_specs=qspec, out_shape=out,
        scratch_shapes=[q2_scratch, pltpu.VMEM((n_blk, 2 * tq, 1), F32),
                        pltpu.VMEM((n_blk, 2 * tq, 2 * LANES), F32)],
        compiler_params=params, name="df_prompt",
    )(slopes, lam_vecs, qdf, kdf, vdf, g_df)
    return o_sb, o_df


def _decode_kernel(pt_ref, slopes_ref, lam_ref, qsb_ref, qdf_ref, nsbk_ref, nsbv_ref, ndfk_ref, ndfv_ref,
                   gsb_ref, gdf_ref, *rest, n_pages, pages_per_step, page, t_new, lam_init):
    pps = pages_per_step
    csbk = rest[0 * pps:1 * pps]
    csbv = rest[1 * pps:2 * pps]
    cdfk = rest[2 * pps:3 * pps]
    cdfv = rest[3 * pps:4 * pps]
    osb_ref, odf_ref = rest[4 * pps:4 * pps + 2]
    qrsb_ref, qrdf_ref, c_ref, m_ref, l_ref, asb_ref, asbt_ref, adf_ref = rest[4 * pps + 2:]

    j = pl.program_id(1)
    n_heads = SEG // LANES
    n_rows = SEG // HEAD_DIM * t_new
    past_len = n_pages * page
    rowq = lax.broadcasted_iota(I32, (n_rows, 1), 0)
    i_q = rowq % t_new
    slope = jnp.zeros((n_rows, 1), F32)
    for h in range(SEG // LANES):
        slope = jnp.where(rowq // (2 * t_new) == h, slopes_ref[h], slope)
    col = lax.broadcasted_iota(I32, (n_rows, page), 1)
    uu = _cumsum_matrix(page)

    def pad_rows(x):
        return jnp.concatenate([x, jnp.zeros((page - t_new, SEG), F32)], axis=0).astype(BF16)

    @pl.when(j == 0)
    def _():
        lane_chunk = lax.broadcasted_iota(I32, (n_rows, SEG), 1) // HEAD_DIM
        own = lane_chunk == lax.broadcasted_iota(I32, (n_rows, SEG), 0) // t_new
        reps = n_rows // t_new
        qrsb_ref[...] = jnp.where(own, jnp.concatenate([qsb_ref[...]] * reps, axis=0), 0.0).astype(BF16)
        qrdf_ref[...] = jnp.where(own, jnp.concatenate([qdf_ref[...]] * reps, axis=0), 0.0).astype(BF16)
        asbt_ref[...] = jnp.zeros_like(asbt_ref)
        strict = col < i_q
        z = _nt_dot(qrsb_ref[...], pad_rows(nsbk_ref[...]))
        cinc = _rev_cumsum(jnp.where(strict, _log1m_beta(z), 0.0), uu)
        a = jnp.where(strict, jnp.exp(z + cinc), 0.0).astype(BF16)
        c_ref[...] = cinc[:, 0:1]
        asb_ref[...] = _dot(a, pad_rows(nsbv_ref[...]))

        s = _nt_dot(qrdf_ref[...], pad_rows(ndfk_ref[...])) - slope * (i_q - col).astype(F32)
        s = jnp.where(col <= i_q, s, NEG_INF)
        m_new = jnp.max(s, axis=-1, keepdims=True)
        p = jnp.exp(s - m_new)
        m_ref[...] = m_new
        l_ref[...] = jnp.sum(p, axis=-1, keepdims=True)
        pv = _dot(p.astype(BF16), pad_rows(ndfv_ref[...]))
        acc = jnp.zeros((n_rows, LANES), F32)
        for h in range(n_heads):
            acc = jnp.where(rowq // (2 * t_new) == h, pv[:, h * LANES:(h + 1) * LANES], acc)
        adf_ref[...] = acc

    qs = qrsb_ref[...]
    qd = qrdf_ref[...]
    z = [_dot(qs, csbk[u][...].astype(BF16)) for u in range(pps)]
    s = []
    for u in range(pps):
        lp = n_pages - 1 - (j * pps + u)
        dist = (past_len + i_q - lp * page - col).astype(F32)
        s.append(_dot(qd, cdfk[u][...].astype(BF16)) - slope * dist)

    cinc = _rev_cumsum(jnp.concatenate([_log1m_beta(zu) for zu in z], axis=0), uu)
    c = c_ref[...]
    acc_t = asbt_ref[...]
    pad = jnp.zeros((LANES - n_rows, page), BF16)
    for u in range(pps):
        cu = cinc[u * n_rows:(u + 1) * n_rows]
        a = jnp.exp(z[u] + c + cu).astype(BF16)
        c = c + cu[:, 0:1]
        acc_t = acc_t + _nt_dot(csbv[u][...].astype(BF16), jnp.concatenate([a, pad], axis=0))
    c_ref[...] = c
    asbt_ref[...] = acc_t

    smax = s[0]
    for u in range(1, pps):
        smax = jnp.maximum(smax, s[u])
    m_i = m_ref[...]
    m_new = jnp.maximum(m_i, jnp.max(smax, axis=-1, keepdims=True))
    alpha = jnp.exp(m_i - m_new)
    head_of_row = rowq // (2 * t_new)
    psum = jnp.zeros((n_rows, page), F32)
    acc = alpha * adf_ref[...]
    for u in range(pps):
        p = jnp.exp(s[u] - m_new)
        psum = psum + p
        p_bd = jnp.concatenate([jnp.where(head_of_row == h, p, 0.0) for h in range(n_heads)], axis=1)
        v_hm = jnp.concatenate([cdfv[u][pl.ds(h, page, stride=n_heads), :] for h in range(n_heads)], axis=0)
        acc = acc + _dot(p_bd.astype(BF16), v_hm.astype(BF16))
    l_ref[...] = alpha * l_ref[...] + jnp.sum(psum, axis=-1, keepdims=True)
    adf_ref[...] = acc
    m_ref[...] = m_new

    @pl.when(j == pl.num_programs(1) - 1)
    def _():
        lane = lax.broadcasted_iota(I32, (t_new, SEG), 1)
        acc = asb_ref[...] + asbt_ref[...].T[:n_rows]
        o = jnp.zeros((t_new, SEG), F32)
        for h in range(SEG // HEAD_DIM):
            o = o + jnp.where(lane // HEAD_DIM == h, acc[h * t_new:(h + 1) * t_new], 0.0)
        o2 = o * o
        ms = jnp.zeros((t_new, SEG), F32)
        for h in range(SEG // HEAD_DIM):
            sel = lane // HEAD_DIM == h
            ms = jnp.where(sel, jnp.sum(jnp.where(sel, o2, 0.0), axis=-1, keepdims=True), ms)
        osb_ref[...] = o * lax.rsqrt(ms * (1.0 / HEAD_DIM) + EPS) * gsb_ref[...]

        lam = _lam_value(lam_ref, lam_init)
        acc = adf_ref[...]
        l_i = l_ref[...]
        outs = []
        for h in range(SEG // LANES):
            r0 = 2 * h * t_new
            o0 = acc[r0:r0 + t_new] / l_i[r0:r0 + t_new]
            o1 = acc[r0 + t_new:r0 + 2 * t_new] / l_i[r0 + t_new:r0 + 2 * t_new]
            outs.append(_rms(o0 - lam * o1, gdf_ref[...]) * (1.0 - lam_init))
        odf_ref[...] = jnp.concatenate(outs, axis=1)


def _decode_attention(page_table, slopes, lam_vecs, qsb, qdf, nsbk, nsbv, ndfk, ndfv, g_sb, g_df,
                      c_sbk, c_sbv, c_dfk, c_dfv, lam_init, pages_per_step=8):
    nb, t_new, _ = qsb.shape
    n_pages = page_table.shape[1]
    page = c_sbk.shape[2]
    pps = pages_per_step
    n_rows = SEG // HEAD_DIM * t_new
    per_b = pl.BlockSpec((None, t_new, SEG), lambda b, j, pt: (b, 0, 0))
    const2 = lambda shape: pl.BlockSpec(shape, lambda b, j, pt: (0, 0))

    def page_spec(u, block):
        zeros = (0,) * (len(block) - 1)
        return pl.BlockSpec(block, lambda b, j, pt: (pt[b, n_pages - 1 - (j * pps + u)],) + zeros)

    page_specs = [page_spec(u, (None, SEG, page)) for u in range(pps)]
    in_specs = ([pl.BlockSpec(memory_space=pltpu.SMEM), const2((4, HEAD_DIM))] + [per_b] * 6
                + [const2((1, SEG)), const2((1, LANES))] + page_specs * 4)
    grid_spec = pltpu.PrefetchScalarGridSpec(
        num_scalar_prefetch=1,
        grid=(nb, n_pages // pps),
        in_specs=in_specs,
        out_specs=[per_b, per_b],
        scratch_shapes=[pltpu.VMEM((n_rows, SEG), BF16), pltpu.VMEM((n_rows, SEG), BF16),
                        pltpu.VMEM((n_rows, 1), F32), pltpu.VMEM((n_rows, 1), F32),
                        pltpu.VMEM((n_rows, 1), F32),
                        pltpu.VMEM((n_rows, SEG), F32), pltpu.VMEM((SEG, LANES), F32),
                        pltpu.VMEM((n_rows, LANES), F32)],
    )
    o = jax.ShapeDtypeStruct((nb, t_new, SEG), F32)
    return pl.pallas_call(
        functools.partial(_decode_kernel, n_pages=n_pages, pages_per_step=pps, page=page,
                          t_new=t_new, lam_init=lam_init),
        grid_spec=grid_spec, out_shape=[o, o],
        compiler_params=pltpu.CompilerParams(dimension_semantics=("arbitrary", "arbitrary"),
                                             vmem_limit_bytes=VMEM_LIMIT),
        name="decode_attn",
    )(page_table, slopes, lam_vecs, qsb, qdf, nsbk, nsbv, ndfk, ndfv, g_sb, g_df,
      *([c_sbk] * pps + [c_sbv] * pps + [c_dfk] * pps + [c_dfv] * pps))


def _gelu_tanh(x):
    return 0.5 * x * (1.0 + jnp.tanh(math.sqrt(2.0 / math.pi) * (x + 0.044715 * x * x * x)))


def _post_kernel(x_ref, msb_ref, mdf_ref, woa_ref, wob_ref, gffn_ref, wup_ref, cw_ref, cb_ref, wdn_ref,
                 gfin_ref, *rest, d_ff, fc, seq_rows):
    if seq_rows is None:
        y_ref, conv_ref, carry_ref = rest
    else:
        p1_ref, p2_ref, y_ref, conv_ref = rest
    tm = x_ref.shape[0]
    h1 = x_ref[...] + _dot(msb_ref[...].astype(BF16), woa_ref[...]) + _dot(mdf_ref[...].astype(BF16), wob_ref[...])
    h2 = _rms(h1, gffn_ref[...]).astype(BF16)
    rowi = lax.broadcasted_iota(I32, (tm, fc), 0)

    if seq_rows is None:
        @pl.when(pl.program_id(1) == 0)
        def _():
            carry_ref[...] = jnp.zeros_like(carry_ref)

    def conv(cols):
        u = _dot(h2, wup_ref[:, cols])
        r1 = pltpu.roll(u, 1, axis=0)
        r2 = pltpu.roll(u, 2, axis=0)
        if seq_rows is None:
            prev = carry_ref[:, cols]
            r1 = jnp.where(rowi == 0, prev[7:8], r1)
            r2 = jnp.where(rowi == 0, prev[6:7], jnp.where(rowi == 1, prev[7:8], r2))
            carry_ref[:, cols] = u[tm - 8:]
            conv_ref[:, cols] = u[tm - 8:]
        else:
            pos = rowi % seq_rows
            r1 = jnp.where(pos == 0, p1_ref[:, cols], r1)
            r2 = jnp.where(pos < 2, p2_ref[:, cols], r2)
            conv_ref[:, cols] = u
        cw = cw_ref[:, cols]
        return cb_ref[:, cols] + cw[2:3] * u + cw[1:2] * r1 + cw[0:1] * r2

    acc = jnp.zeros((tm, x_ref.shape[1]), F32)
    for c in range(d_ff // fc):
        gate = conv(slice(c * fc, (c + 1) * fc))
        val = conv(slice(d_ff + c * fc, d_ff + (c + 1) * fc))
        act = (_gelu_tanh(gate) * val).astype(BF16)
        acc = acc + _dot(act, wdn_ref[c * fc:(c + 1) * fc, :])
    y_ref[...] = _rms(h1 + acc, gfin_ref[...])


def _post(x, msb, mdf, woa, wob, gffn, wup, cw, cb, wdn, gfin, prev=None, tm=256, fc=256):
    d = x.shape[-1]
    d_ff = wdn.shape[0]
    weights = (woa, wob, gffn, wup, cw, cb, wdn, gfin)
    if prev is None:
        b, t, _ = x.shape
        grid = (b, t // tm)
        act = lambda w: pl.BlockSpec((None, tm, w), lambda bi, ti: (bi, ti, 0))
        const = lambda a: pl.BlockSpec(a.shape, lambda bi, ti: (0,) * a.ndim)
        in_specs = [act(d), act(SEG), act(SEG)] + [const(a) for a in weights]
        out_specs = [act(d), pl.BlockSpec((None, 8, 2 * d_ff), lambda bi, ti: (bi, 0, 0))]
        out_shape = [jax.ShapeDtypeStruct((b, t, d), F32), jax.ShapeDtypeStruct((b, 8, 2 * d_ff), F32)]
        scratch = [pltpu.VMEM((8, 2 * d_ff), F32)]
        args = (x, msb, mdf) + weights
        kern = functools.partial(_post_kernel, d_ff=d_ff, fc=fc, seq_rows=None)
        name = "post_prompt"
    else:
        p1, p2, seq_rows = prev
        m = x.shape[0]
        grid = (m // tm, 1)
        act = lambda w: pl.BlockSpec((tm, w), lambda mi, ti: (mi, 0))
        const = lambda a: pl.BlockSpec(a.shape, lambda mi, ti: (0,) * a.ndim)
        in_specs = ([act(d), act(SEG), act(SEG)] + [const(a) for a in weights]
                    + [act(2 * d_ff), act(2 * d_ff)])
        out_specs = [act(d), act(2 * d_ff)]
        out_shape = [jax.ShapeDtypeStruct((m, d), F32), jax.ShapeDtypeStruct((m, 2 * d_ff), F32)]
        scratch = []
        args = (x, msb, mdf) + weights + (p1, p2)
        kern = functools.partial(_post_kernel, d_ff=d_ff, fc=fc, seq_rows=seq_rows)
        name = "post_sample"
    return pl.pallas_call(
        kern, grid=grid, in_specs=in_specs, out_specs=out_specs, out_shape=out_shape,
        scratch_shapes=scratch,
        compiler_params=pltpu.CompilerParams(dimension_semantics=("arbitrary", "arbitrary"),
                                             vmem_limit_bytes=56 * 1024 * 1024),
        name=name,
    )(*args)


def kernel(x_prompt, x_sample, cache_sb_k, cache_sb_v, cache_df_k, cache_df_v, state_conv, page_table,
           norm_mix, w_in, sb_head_norm, df_head_norm, lam_q1, lam_k1, lam_q2, lam_k2, w_o,
           norm_ffn, w_up, conv_w, conv_b, w_down, norm_final):
    depth = w_in.shape[0]
    assert depth == 1, "single-layer step"
    b, t, d = x_prompt.shape
    nb, t_new, _ = x_sample.shape
    n_pool, page = cache_sb_k.shape[1], cache_sb_k.shape[2]
    d_ff = w_down.shape[1]
    sb_heads = cache_sb_k.shape[3]
    df_heads = cache_df_k.shape[3]
    lam_init = 0.8 - 0.6 * math.exp(-0.3 * 0)
    slopes = jnp.asarray(2.0 ** (-8.0 * np.arange(1, df_heads + 1) / df_heads), dtype=F32)

    g_mix = norm_mix[0].reshape(1, d)
    w_in_b = w_in[0].astype(BF16)
    g_sb = jnp.tile(sb_head_norm[0], LANES // HEAD_DIM).reshape(1, LANES)
    g_sb_full = jnp.tile(sb_head_norm[0], SEG // HEAD_DIM).reshape(1, SEG)
    g_df = df_head_norm[0].reshape(1, LANES)
    lam_vecs = jnp.stack([lam_q1[0], lam_k1[0], lam_q2[0], lam_k2[0]]).astype(F32)
    wo_b = w_o[0].astype(BF16)
    post_w = (wo_b[:SEG], wo_b[SEG:], norm_ffn[0].reshape(1, d), w_up[0].astype(BF16), conv_w[0],
              conv_b[0].reshape(1, 2 * d_ff), w_down[0].astype(BF16), norm_final.reshape(1, d))

    (p_sbk, p_sbv, p_dfk, p_dfv, qsb, ksb, vsb, qdf, kdf, vdf) = _proj(
        x_prompt.reshape(b * t, d), g_mix, w_in_b, BF16)
    r3 = lambda a: a.reshape(b, t, SEG)
    o_sb, o_df = _prompt_attention(r3(qsb), r3(ksb), r3(vsb), r3(qdf), r3(kdf), r3(vdf),
                                   g_sb, g_df, slopes, lam_vecs, lam_init)
    y_prompt, conv_tail = _post(x_prompt, o_sb, o_df, *post_w)
    p_conv = conv_tail[:, 8 - (CONV_W - 1):]

    (s_sbk, s_sbv, s_dfk, s_dfv, sqsb, _, _, sqdf, _, _) = _proj(
        x_sample.reshape(nb * t_new, d), g_mix, w_in_b, F32)
    s3 = lambda a: a.reshape(nb, t_new, SEG)
    pos_minor = lambda c: jnp.moveaxis(c.reshape(n_pool, page, SEG), 1, 2)
    so_sb, so_df = _decode_attention(
        page_table, slopes, lam_vecs, s3(sqsb), s3(sqdf), s3(s_sbk), s3(s_sbv), s3(s_dfk), s3(s_dfv),
        g_sb_full, g_df, pos_minor(cache_sb_k), pos_minor(cache_sb_v), pos_minor(cache_df_k),
        cache_df_v.reshape(n_pool, page * df_heads, 2 * HEAD_DIM), lam_init)
    st = state_conv[0]
    zeros = jnp.zeros((nb, t_new - 1, 2 * d_ff), F32)
    prev1 = jnp.concatenate([st[:, 1:2], zeros], axis=1).reshape(nb * t_new, 2 * d_ff)
    prev2 = jnp.concatenate([st, zeros[:, 1:]], axis=1).reshape(nb * t_new, 2 * d_ff)
    y_sample, u_s = _post(x_sample.reshape(nb * t_new, d), so_sb.reshape(nb * t_new, SEG),
                          so_df.reshape(nb * t_new, SEG), *post_w, prev=(prev1, prev2, t_new))
    s_conv = u_s.reshape(nb, t_new, 2 * d_ff)[:, t_new - (CONV_W - 1):]

    return (y_prompt, y_sample.reshape(nb, t_new, d),
            p_sbk.reshape(1, b, t, sb_heads, HEAD_DIM), p_sbv.reshape(1, b, t, sb_heads, HEAD_DIM),
            p_dfk.reshape(1, b, t, df_heads, 2, HEAD_DIM), p_dfv.reshape(1, b, t, df_heads, 2 * HEAD_DIM),
            p_conv[None],
            s_sbk.reshape(1, nb, t_new, sb_heads, HEAD_DIM), s_sbv.reshape(1, nb, t_new, sb_heads, HEAD_DIM),
            s_dfk.reshape(1, nb, t_new, df_heads, 2, HEAD_DIM), s_dfv.reshape(1, nb, t_new, df_heads, 2 * HEAD_DIM),
            s_conv[None])
```

```python
import functools
import math

import numpy as np
import jax
import jax.numpy as jnp
from jax import lax
from jax.experimental import pallas as pl
from jax.experimental.pallas import tpu as pltpu

F32 = jnp.float32
BF16 = jnp.bfloat16
I32 = jnp.int32

EPS = 1e-6
HEAD_DIM = 64
LANES = 128
SEG = 512
N_SEG = 6
CONV_W = 3
NEG_INF = float("-inf")
VMEM_LIMIT = 48 * 1024 * 1024
PROMPT_TM = 512


def _rms(x, g):
    ms = jnp.mean(x * x, axis=-1, keepdims=True)
    return x * lax.rsqrt(ms + EPS) * g


def _nt_dot(a, b):
    return lax.dot_general(a, b, (((1,), (1,)), ((), ())), preferred_element_type=F32)


def _dot(a, b):
    return jnp.dot(a, b, preferred_element_type=F32)


def _cumsum_matrix(tk):
    j = lax.broadcasted_iota(I32, (2 * tk, tk), 0)
    s = lax.broadcasted_iota(I32, (2 * tk, tk), 1)
    j = jnp.where(j >= tk, j - tk, j)
    return jnp.where(j >= s, 1.0, 0.0).astype(BF16)


def _rev_cumsum(l, uu):
    hi = l.astype(BF16)
    lo = (l - hi.astype(F32)).astype(BF16)
    return _dot(jnp.concatenate([hi, lo], axis=1), uu)


def _log1m_beta(z):
    return -(jnp.maximum(z, 0.0) + jnp.log(1.0 + jnp.exp(-jnp.abs(z))))


def _proj_kernel(x_ref, g_ref, w_ref, sbk_ref, sbv_ref, dfk_ref, dfv_ref,
                 qsb_ref, ksb_ref, vsb_ref, qdf_ref, kdf_ref, vdf_ref, *, stored_layout):
    h = _rms(x_ref[...], g_ref[...]).astype(BF16)
    tm = h.shape[0]

    def seg(j):
        return _dot(h, w_ref[:, j * SEG:(j + 1) * SEG])

    def put(ref, val):
        ref[...] = val.T if stored_layout else val

    scale = HEAD_DIM ** -0.5
    qsb_ref[...] = (seg(0) * scale).astype(qsb_ref.dtype)
    k = seg(1)
    put(sbk_ref, k)
    ksb_ref[...] = k.astype(BF16)
    v = seg(2)
    put(sbv_ref, v)
    vsb_ref[...] = v.astype(BF16)
    qdf_ref[...] = (seg(3) * scale).astype(qdf_ref.dtype)
    k = seg(4)
    put(dfk_ref, k)
    kdf_ref[...] = k.astype(BF16)
    v = seg(5)
    if stored_layout:
        n_heads = SEG // LANES
        for hd in range(n_heads):
            dfv_ref[pl.ds(hd, tm, stride=n_heads), :] = v[:, hd * LANES:(hd + 1) * LANES]
    else:
        dfv_ref[...] = v
    vdf_ref[...] = v.astype(BF16)


def _proj(x, g, w, q_dtype, stored_layout, tm):
    b, t, d = x.shape
    grid = (b, t // tm)
    rows = pl.BlockSpec((None, tm, SEG), lambda bi, ti: (bi, ti, 0))
    const = lambda bi, ti: (0, 0)
    f32o = jax.ShapeDtypeStruct((b, t, SEG), F32)
    b16o = jax.ShapeDtypeStruct((b, t, SEG), BF16)
    qo = jax.ShapeDtypeStruct((b, t, SEG), q_dtype)
    if stored_layout:
        n_heads = SEG // LANES
        tspec = pl.BlockSpec((None, SEG, tm), lambda bi, ti: (bi, 0, ti))
        to = jax.ShapeDtypeStruct((b, SEG, t), F32)
        kv_specs = [tspec, tspec, tspec, pl.BlockSpec((None, tm * n_heads, LANES), lambda bi, ti: (bi, ti, 0))]
        kv_shapes = [to, to, to, jax.ShapeDtypeStruct((b, t * n_heads, LANES), F32)]
    else:
        kv_specs = [rows] * 4
        kv_shapes = [f32o] * 4
    return pl.pallas_call(
        functools.partial(_proj_kernel, stored_layout=stored_layout),
        grid=grid,
        in_specs=[pl.BlockSpec((None, tm, d), lambda bi, ti: (bi, ti, 0)), pl.BlockSpec((1, d), const),
                  pl.BlockSpec((d, N_SEG * SEG), const, pipeline_mode=pl.Buffered(1))],
        out_specs=kv_specs + [rows] * 6,
        out_shape=kv_shapes + [qo, b16o, b16o, qo, b16o, b16o],
        compiler_params=pltpu.CompilerParams(dimension_semantics=("arbitrary", "arbitrary"),
                                             vmem_limit_bytes=VMEM_LIMIT),
        name="proj",
    )(x, g, w)


def _split_lane_halves_rows(x):
    lo = lax.broadcasted_iota(I32, x.shape, 1) < HEAD_DIM
    zero = jnp.zeros_like(x)
    return jnp.concatenate([jnp.where(lo, x, zero), jnp.where(lo, zero, x)], axis=0)


F32_EXP_ZERO = -104.0


def _sb_prompt_kernel(q_ref, k_ref, v_ref, g_ref, o_ref, q2_ref, c_ref, acc_ref, *, tq, tk):
    qi = pl.program_id(1)
    n_blk = SEG // LANES
    for p in range(n_blk):
        q2_ref[p] = _split_lane_halves_rows(q_ref[:, p * LANES:(p + 1) * LANES])
    c_ref[...] = jnp.zeros_like(c_ref)
    acc_ref[...] = jnp.zeros_like(acc_ref)
    uu = _cumsum_matrix(tk)
    row = lax.broadcasted_iota(I32, (2 * tq, tk), 0)
    col = lax.broadcasted_iota(I32, (2 * tq, tk), 1)
    qpos = qi * tq + jnp.where(row >= tq, row - tq, row)

    def block(kj, masked):
        start = pl.multiple_of(kj * tk, tk)
        if masked:
            strict = start + col < qpos
        for p in range(n_blk):
            lanes = slice(p * LANES, (p + 1) * LANES)
            z = _nt_dot(q2_ref[p], k_ref[pl.ds(start, tk), lanes])
            l = _log1m_beta(z)
            if masked:
                l = jnp.where(strict, l, 0.0)
            cinc = _rev_cumsum(l, uu)
            c = c_ref[p]
            a = jnp.exp(z + c + cinc)
            if masked:
                a = jnp.where(strict, a, 0.0)
            c_ref[p] = c + cinc[:, 0:1]
            ab = a.astype(BF16)
            a_cat = jnp.concatenate([ab[:tq], ab[tq:]], axis=1)
            acc_ref[p] += _dot(a_cat, _split_lane_halves_rows(v_ref[pl.ds(start, tk), lanes]))

    n_diag = tq // tk
    top = (qi + 1) * n_diag - 1
    for d in range(n_diag):
        block(top - d, True)

    def c_max():
        c = c_ref[0]
        for p in range(1, n_blk):
            c = jnp.maximum(c, c_ref[p])
        return jnp.max(c)

    def body(carry):
        kj, _ = carry
        block(kj, False)
        return kj - 1, c_max()

    lax.while_loop(lambda carry: (carry[0] >= 0) & (carry[1] > F32_EXP_ZERO), body,
                   (top - n_diag, c_max()))

    lo = lax.broadcasted_iota(I32, (tq, LANES), 1) < HEAD_DIM
    for p in range(n_blk):
        o = acc_ref[p]
        o2 = o * o
        s_lo = jnp.sum(jnp.where(lo, o2, 0.0), axis=-1, keepdims=True)
        s_hi = jnp.sum(jnp.where(lo, 0.0, o2), axis=-1, keepdims=True)
        ms = jnp.where(lo, s_lo, s_hi) * (1.0 / HEAD_DIM)
        o_ref[:, p * LANES:(p + 1) * LANES] = (o * lax.rsqrt(ms + EPS) * g_ref[...]).astype(o_ref.dtype)


def _lam_value(lam_ref, lam_init):
    lv = lam_ref[...]
    s1 = jnp.sum(lv[0:1] * lv[1:2], axis=-1, keepdims=True)
    s2 = jnp.sum(lv[2:3] * lv[3:4], axis=-1, keepdims=True)
    return jnp.exp(s1) - jnp.exp(s2) + lam_init


def _df_prompt_kernel(slopes_ref, lam_ref, q_ref, k_ref, v_ref, g_ref, o_ref, q2_ref, m_ref, acc_ref,
                      *, tq, tk, lam_init):
    qi = pl.program_id(1)
    n_heads = SEG // LANES
    for h in range(n_heads):
        q2_ref[h] = _split_lane_halves_rows(q_ref[:, h * LANES:(h + 1) * LANES])
    m_ref[...] = jnp.full_like(m_ref, NEG_INF)
    acc_ref[...] = jnp.zeros_like(acc_ref)
    row = lax.broadcasted_iota(I32, (2 * tq, tk), 0)
    col = lax.broadcasted_iota(I32, (2 * tq, tk), 1)
    rel = jnp.where(row >= tq, row - tq, row) - col
    ones = jnp.ones((tk, LANES), BF16)

    def block(kj, masked):
        start = pl.multiple_of(kj * tk, tk)
        dist_i = rel + (qi * tq - start)
        dist = dist_i.astype(F32)
        for h in range(n_heads):
            lanes = slice(h * LANES, (h + 1) * LANES)
            s = _nt_dot(q2_ref[h], k_ref[pl.ds(start, tk), lanes]) - slopes_ref[h] * dist
            if masked:
                s = jnp.where(dist_i >= 0, s, NEG_INF)
            m_i = m_ref[h]
            m_new = jnp.maximum(m_i, jnp.max(s, axis=-1, keepdims=True))
            alpha = jnp.exp(m_i - m_new)
            p = jnp.exp(s - m_new).astype(BF16)
            v_aug = jnp.concatenate([v_ref[pl.ds(start, tk), lanes], ones], axis=1)
            acc_ref[h] = alpha * acc_ref[h] + _dot(p, v_aug)
            m_ref[h] = m_new

    top = ((qi + 1) * tq - 1) // tk
    block(top, True)
    lax.fori_loop(0, top, lambda i, _: block(top - 1 - i, False), None)

    lam = _lam_value(lam_ref, lam_init)
    for h in range(n_heads):
        acc = acc_ref[h]
        r = acc[:, :LANES] / acc[:, LANES:]
        o = r[:tq] - lam * r[tq:]
        o_ref[:, h * LANES:(h + 1) * LANES] = (_rms(o, g_ref[...]) * (1.0 - lam_init)).astype(o_ref.dtype)


def _prompt_attention(qsb, ksb, vsb, qdf, kdf, vdf, g_sb, g_df, slopes, lam_vecs, lam_init,
                      tq_sb=256, tk_sb=256, tq_df=256, tk_df=1024):
    b, t, _ = qsb.shape
    n_blk = SEG // LANES
    qspec = lambda tq: pl.BlockSpec((None, tq, SEG), lambda bi, qi: (bi, qi, 0))
    kvspec = pl.BlockSpec((None, t, SEG), lambda bi, qi: (bi, 0, 0))
    gspec = pl.BlockSpec((1, LANES), lambda bi, qi: (0, 0))
    out = jax.ShapeDtypeStruct((b, t, SEG), BF16)
    params = pltpu.CompilerParams(dimension_semantics=("arbitrary",) * 2, vmem_limit_bytes=VMEM_LIMIT)
    q2_scratch = lambda tq: pltpu.VMEM((n_blk, 2 * tq, LANES), BF16)
    o_sb = pl.pallas_call(
        functools.partial(_sb_prompt_kernel, tq=tq_sb, tk=tk_sb),
        grid=(b, t // tq_sb), in_specs=[qspec(tq_sb), kvspec, kvspec, gspec], out_specs=qspec(tq_sb),
        out_shape=out,
        scratch_shapes=[q2_scratch(tq_sb), pltpu.VMEM((n_blk, 2 * tq_sb, 1), F32),
                        pltpu.VMEM((n_blk, tq_sb, LANES), F32)],
        compiler_params=params, name="sb_prompt",
    )(qsb, ksb, vsb, g_sb)
    o_df = pl.pallas_call(
        functools.partial(_df_prompt_kernel, tq=tq_df, tk=tk_df, lam_init=lam_init),
        grid=(b, t // tq_df),
        in_specs=[pl.BlockSpec(memory_space=pltpu.SMEM),
                  pl.BlockSpec((4, HEAD_DIM), lambda bi, qi: (0, 0)),
                  qspec(tq_df), kvspec, kvspec, gspec],
        out_specs=qspec(tq_df), out_shape=out,
        scratch_shapes=[q2_scratch(tq_df), pltpu.VMEM((n_blk, 2 * tq_df, 1), F32),
                        pltpu.VMEM((n_blk, 2 * tq_df, 2 * LANES), F32)],
        compiler_params=params, name="df_prompt",
    )(slopes, lam_vecs, qdf, kdf, vdf, g_df)
    return o_sb, o_df


def _decode_kernel(pt_ref, slopes_ref, lam_ref, qsb_ref, qdf_ref, nsbk_ref, nsbv_ref, ndfk_ref, ndfv_ref,
                   gsb_ref, gdf_ref, *rest, n_pages, pages_per_step, page, t_new, lam_init):
    pps = pages_per_step
    csbk = rest[0 * pps:1 * pps]
    csbv = rest[1 * pps:2 * pps]
    cdfk = rest[2 * pps:3 * pps]
    cdfv = rest[3 * pps:4 * pps]
    osb_ref, odf_ref = rest[4 * pps:4 * pps + 2]
    qrsb_ref, qrdf_ref, c_ref, m_ref, l_ref, asb_ref, asbt_ref, adf_ref = rest[4 * pps + 2:]

    j = pl.program_id(1)
    n_heads = SEG // LANES
    n_rows = SEG // HEAD_DIM * t_new
    past_len = n_pages * page
    rowq = lax.broadcasted_iota(I32, (n_rows, 1), 0)
    i_q = rowq % t_new
    slope = jnp.zeros((n_rows, 1), F32)
    for h in range(SEG // LANES):
        slope = jnp.where(rowq // (2 * t_new) == h, slopes_ref[h], slope)
    col = lax.broadcasted_iota(I32, (n_rows, page), 1)
    uu = _cumsum_matrix(page)

    def pad_rows(x):
        return jnp.concatenate([x, jnp.zeros((page - t_new, SEG), F32)], axis=0).astype(BF16)

    @pl.when(j == 0)
    def _():
        lane_chunk = lax.broadcasted_iota(I32, (n_rows, SEG), 1) // HEAD_DIM
        own = lane_chunk == lax.broadcasted_iota(I32, (n_rows, SEG), 0) // t_new
        reps = n_rows // t_new
        qrsb_ref[...] = jnp.where(own, jnp.concatenate([qsb_ref[...]] * reps, axis=0), 0.0).astype(BF16)
        qrdf_ref[...] = jnp.where(own, jnp.concatenate([qdf_ref[...]] * reps, axis=0), 0.0).astype(BF16)
        asbt_ref[...] = jnp.zeros_like(asbt_ref)
        strict = col < i_q
        z = _nt_dot(qrsb_ref[...], pad_rows(nsbk_ref[...]))
        cinc = _rev_cumsum(jnp.where(strict, _log1m_beta(z), 0.0), uu)
        a = jnp.where(strict, jnp.exp(z + cinc), 0.0).astype(BF16)
        c_ref[...] = cinc[:, 0:1]
        asb_ref[...] = _dot(a, pad_rows(nsbv_ref[...]))

        s = _nt_dot(qrdf_ref[...], pad_rows(ndfk_ref[...])) - slope * (i_q - col).astype(F32)
        s = jnp.where(col <= i_q, s, NEG_INF)
        m_new = jnp.max(s, axis=-1, keepdims=True)
        p = jnp.exp(s - m_new)
        m_ref[...] = m_new
        l_ref[...] = jnp.sum(p, axis=-1, keepdims=True)
        pv = _dot(p.astype(BF16), pad_rows(ndfv_ref[...]))
        acc = jnp.zeros((n_rows, LANES), F32)
        for h in range(n_heads):
            acc = jnp.where(rowq // (2 * t_new) == h, pv[:, h * LANES:(h + 1) * LANES], acc)
        adf_ref[...] = acc

    qs = qrsb_ref[...]
    qd = qrdf_ref[...]
    z = [_dot(qs, csbk[u][...].astype(BF16)) for u in range(pps)]
    s = []
    for u in range(pps):
        lp = n_pages - 1 - (j * pps + u)
        dist = (past_len + i_q - lp * page - col).astype(F32)
        s.append(_dot(qd, cdfk[u][...].astype(BF16)) - slope * dist)

    cinc = _rev_cumsum(jnp.concatenate([_log1m_beta(zu) for zu in z], axis=0), uu)
    c = c_ref[...]
    acc_t = asbt_ref[...]
    pad = jnp.zeros((LANES - n_rows, page), BF16)
    for u in range(pps):
        cu = cinc[u * n_rows:(u + 1) * n_rows]
        a = jnp.exp(z[u] + c + cu).astype(BF16)
        c = c + cu[:, 0:1]
        acc_t = acc_t + _nt_dot(csbv[u][...].astype(BF16), jnp.concatenate([a, pad], axis=0))
    c_ref[...] = c
    asbt_ref[...] = acc_t

    smax = s[0]
    for u in range(1, pps):
        smax = jnp.maximum(smax, s[u])
    m_i = m_ref[...]
    m_new = jnp.maximum(m_i, jnp.max(smax, axis=-1, keepdims=True))
    alpha = jnp.exp(m_i - m_new)
    head_of_row = rowq // (2 * t_new)
    psum = jnp.zeros((n_rows, page), F32)
    acc = alpha * adf_ref[...]
    for u in range(pps):
        p = jnp.exp(s[u] - m_new)
        psum = psum + p
        p_bd = jnp.concatenate([jnp.where(head_of_row == h, p, 0.0) for h in range(n_heads)], axis=1)
        v_hm = jnp.concatenate([cdfv[u][pl.ds(h, page, stride=n_heads), :] for h in range(n_heads)], axis=0)
        acc = acc + _dot(p_bd.astype(BF16), v_hm.astype(BF16))
    l_ref[...] = alpha * l_ref[...] + jnp.sum(psum, axis=-1, keepdims=True)
    adf_ref[...] = acc
    m_ref[...] = m_new

    @pl.when(j == pl.num_programs(1) - 1)
    def _():
        lane = lax.broadcasted_iota(I32, (t_new, SEG), 1)
        acc = asb_ref[...] + asbt_ref[...].T[:n_rows]
        o = jnp.zeros((t_new, SEG), F32)
        for h in range(SEG // HEAD_DIM):
            o = o + jnp.where(lane // HEAD_DIM == h, acc[h * t_new:(h + 1) * t_new], 0.0)
        o2 = o * o
        ms = jnp.zeros((t_new, SEG), F32)
        for h in range(SEG // HEAD_DIM):
            sel = lane // HEAD_DIM == h
            ms = jnp.where(sel, jnp.sum(jnp.where(sel, o2, 0.0), axis=-1, keepdims=True), ms)
        osb_ref[...] = o * lax.rsqrt(ms * (1.0 / HEAD_DIM) + EPS) * gsb_ref[...]

        lam = _lam_value(lam_ref, lam_init)
        acc = adf_ref[...]
        l_i = l_ref[...]
        outs = []
        for h in range(SEG // LANES):
            r0 = 2 * h * t_new
            o0 = acc[r0:r0 + t_new] / l_i[r0:r0 + t_new]
            o1 = acc[r0 + t_new:r0 + 2 * t_new] / l_i[r0 + t_new:r0 + 2 * t_new]
            outs.append(_rms(o0 - lam * o1, gdf_ref[...]) * (1.0 - lam_init))
        odf_ref[...] = jnp.concatenate(outs, axis=1)


def _decode_attention(page_table, slopes, lam_vecs, qsb, qdf, nsbk, nsbv, ndfk, ndfv, g_sb, g_df,
                      c_sbk, c_sbv, c_dfk, c_dfv, lam_init, pages_per_step=16):
    nb, t_new, _ = qsb.shape
    n_pages = page_table.shape[1]
    page = c_sbk.shape[2]
    pps = pages_per_step
    n_rows = SEG // HEAD_DIM * t_new
    per_b = pl.BlockSpec((None, t_new, SEG), lambda b, j, pt: (b, 0, 0))
    const2 = lambda shape: pl.BlockSpec(shape, lambda b, j, pt: (0, 0))

    def page_spec(u, block):
        zeros = (0,) * (len(block) - 1)
        return pl.BlockSpec(block, lambda b, j, pt: (pt[b, n_pages - 1 - (j * pps + u)],) + zeros)

    page_specs = [page_spec(u, (None, SEG, page)) for u in range(pps)]
    in_specs = ([pl.BlockSpec(memory_space=pltpu.SMEM), const2((4, HEAD_DIM))] + [per_b] * 6
                + [const2((1, SEG)), const2((1, LANES))] + page_specs * 4)
    grid_spec = pltpu.PrefetchScalarGridSpec(
        num_scalar_prefetch=1,
        grid=(nb, n_pages // pps),
        in_specs=in_specs,
        out_specs=[per_b, per_b],
        scratch_shapes=[pltpu.VMEM((n_rows, SEG), BF16), pltpu.VMEM((n_rows, SEG), BF16),
                        pltpu.VMEM((n_rows, 1), F32), pltpu.VMEM((n_rows, 1), F32),
                        pltpu.VMEM((n_rows, 1), F32),
                        pltpu.VMEM((n_rows, SEG), F32), pltpu.VMEM((SEG, LANES), F32),
                        pltpu.VMEM((n_rows, LANES), F32)],
    )
    o = jax.ShapeDtypeStruct((nb, t_new, SEG), F32)
    return pl.pallas_call(
        functools.partial(_decode_kernel, n_pages=n_pages, pages_per_step=pps, page=page,
                          t_new=t_new, lam_init=lam_init),
        grid_spec=grid_spec, out_shape=[o, o],
        compiler_params=pltpu.CompilerParams(dimension_semantics=("arbitrary", "arbitrary"),
                                             vmem_limit_bytes=VMEM_LIMIT),
        name="decode_attn",
    )(page_table, slopes, lam_vecs, qsb, qdf, nsbk, nsbv, ndfk, ndfv, g_sb, g_df,
      *([c_sbk] * pps + [c_sbv] * pps + [c_dfk] * pps + [c_dfv] * pps))


def _gelu_tanh(x):
    return 0.5 * x * (1.0 + jnp.tanh(math.sqrt(2.0 / math.pi) * (x + 0.044715 * x * x * x)))


def _post_kernel(x_ref, msb_ref, mdf_ref, woa_ref, wob_ref, gffn_ref, wup_ref, cw_ref, cb_ref, wdn_ref,
                 gfin_ref, *rest, d_ff, fc, seq_rows):
    if seq_rows is None:
        y_ref, conv_ref, carry_ref = rest
    else:
        p1_ref, p2_ref, y_ref, conv_ref = rest
    tm = x_ref.shape[0]
    h1 = x_ref[...] + _dot(msb_ref[...].astype(BF16), woa_ref[...]) + _dot(mdf_ref[...].astype(BF16), wob_ref[...])
    h2 = _rms(h1, gffn_ref[...]).astype(BF16)
    rowi = lax.broadcasted_iota(I32, (tm, fc), 0)

    if seq_rows is None:
        @pl.when(pl.program_id(1) == 0)
        def _():
            carry_ref[...] = jnp.zeros_like(carry_ref)

    def conv(cols):
        u = _dot(h2, wup_ref[:, cols])
        r1 = pltpu.roll(u, 1, axis=0)
        r2 = pltpu.roll(u, 2, axis=0)
        if seq_rows is None:
            prev = carry_ref[:, cols]
            r1 = jnp.where(rowi == 0, prev[7:8], r1)
            r2 = jnp.where(rowi == 0, prev[6:7], jnp.where(rowi == 1, prev[7:8], r2))
            carry_ref[:, cols] = u[tm - 8:]
            conv_ref[:, cols] = u[tm - 8:]
        else:
            pos = rowi % seq_rows
            r1 = jnp.where(pos == 0, p1_ref[:, cols], r1)
            r2 = jnp.where(pos < 2, p2_ref[:, cols], r2)
            conv_ref[:, cols] = u
        cw = cw_ref[:, cols]
        return cb_ref[:, cols] + cw[2:3] * u + cw[1:2] * r1 + cw[0:1] * r2

    acc = jnp.zeros((tm, x_ref.shape[1]), F32)
    for c in range(d_ff // fc):
        gate = conv(slice(c * fc, (c + 1) * fc))
        val = conv(slice(d_ff + c * fc, d_ff + (c + 1) * fc))
        act = (_gelu_tanh(gate) * val).astype(BF16)
        acc = acc + _dot(act, wdn_ref[c * fc:(c + 1) * fc, :])
    y_ref[...] = _rms(h1 + acc, gfin_ref[...])


def _post(x, msb, mdf, woa, wob, gffn, wup, cw, cb, wdn, gfin, prev=None, *, tm, fc=256):
    d = x.shape[-1]
    d_ff = wdn.shape[0]
    weights = (woa, wob, gffn, wup, cw, cb, wdn, gfin)
    if prev is None:
        b, t, _ = x.shape
        grid = (b, t // tm)
        act = lambda w: pl.BlockSpec((None, tm, w), lambda bi, ti: (bi, ti, 0))
        const = lambda a: pl.BlockSpec(a.shape, lambda bi, ti: (0,) * a.ndim, pipeline_mode=pl.Buffered(1))
        in_specs = [act(d), act(SEG), act(SEG)] + [const(a) for a in weights]
        out_specs = [act(d), pl.BlockSpec((None, 8, 2 * d_ff), lambda bi, ti: (bi, 0, 0))]
        out_shape = [jax.ShapeDtypeStruct((b, t, d), F32), jax.ShapeDtypeStruct((b, 8, 2 * d_ff), F32)]
        scratch = [pltpu.VMEM((8, 2 * d_ff), F32)]
        args = (x, msb, mdf) + weights
        kern = functools.partial(_post_kernel, d_ff=d_ff, fc=fc, seq_rows=None)
        name = "post_prompt"
    else:
        p1, p2, seq_rows = prev
        m = x.shape[0]
        grid = (m // tm, 1)
        act = lambda w: pl.BlockSpec((tm, w), lambda mi, ti: (mi, 0))
        const = lambda a: pl.BlockSpec(a.shape, lambda mi, ti: (0,) * a.ndim)
        in_specs = ([act(d), act(SEG), act(SEG)] + [const(a) for a in weights]
                    + [act(2 * d_ff), act(2 * d_ff)])
        out_specs = [act(d), act(2 * d_ff)]
        out_shape = [jax.ShapeDtypeStruct((m, d), F32), jax.ShapeDtypeStruct((m, 2 * d_ff), F32)]
        scratch = []
        args = (x, msb, mdf) + weights + (p1, p2)
        kern = functools.partial(_post_kernel, d_ff=d_ff, fc=fc, seq_rows=seq_rows)
        name = "post_sample"
    return pl.pallas_call(
        kern, grid=grid, in_specs=in_specs, out_specs=out_specs, out_shape=out_shape,
        scratch_shapes=scratch,
        compiler_params=pltpu.CompilerParams(dimension_semantics=("arbitrary", "arbitrary"),
                                             vmem_limit_bytes=56 * 1024 * 1024),
        name=name,
    )(*args)


def kernel(x_prompt, x_sample, cache_sb_k, cache_sb_v, cache_df_k, cache_df_v, state_conv, page_table,
           norm_mix, w_in, sb_head_norm, df_head_norm, lam_q1, lam_k1, lam_q2, lam_k2, w_o,
           norm_ffn, w_up, conv_w, conv_b, w_down, norm_final):
    depth = w_in.shape[0]
    assert depth == 1, "single-layer step"
    b, t, d = x_prompt.shape
    nb, t_new, _ = x_sample.shape
    n_pool, page = cache_sb_k.shape[1], cache_sb_k.shape[2]
    d_ff = w_down.shape[1]
    sb_heads = cache_sb_k.shape[3]
    df_heads = cache_df_k.shape[3]
    lam_init = 0.8 - 0.6 * math.exp(-0.3 * 0)
    slopes = jnp.asarray(2.0 ** (-8.0 * np.arange(1, df_heads + 1) / df_heads), dtype=F32)

    g_mix = norm_mix[0].reshape(1, d)
    w_in_b = w_in[0].astype(BF16)
    g_sb = jnp.tile(sb_head_norm[0], LANES // HEAD_DIM).reshape(1, LANES)
    g_sb_full = jnp.tile(sb_head_norm[0], SEG // HEAD_DIM).reshape(1, SEG)
    g_df = df_head_norm[0].reshape(1, LANES)
    lam_vecs = jnp.stack([lam_q1[0], lam_k1[0], lam_q2[0], lam_k2[0]]).astype(F32)
    wo_b = w_o[0].astype(BF16)
    post_w = (wo_b[:SEG], wo_b[SEG:], norm_ffn[0].reshape(1, d), w_up[0].astype(BF16), conv_w[0],
              conv_b[0].reshape(1, 2 * d_ff), w_down[0].astype(BF16), norm_final.reshape(1, d))

    (p_sbk, p_sbv, p_dfk, p_dfv, qsb, ksb, vsb, qdf, kdf, vdf) = _proj(
        x_prompt, g_mix, w_in_b, BF16, stored_layout=True, tm=PROMPT_TM)
    p_sbk = jnp.moveaxis(p_sbk.reshape(b, sb_heads, HEAD_DIM, t), 3, 1)
    p_sbv = jnp.moveaxis(p_sbv.reshape(b, sb_heads, HEAD_DIM, t), 3, 1)
    p_dfk = jnp.moveaxis(p_dfk.reshape(b, df_heads, 2, HEAD_DIM, t), 4, 1)
    o_sb, o_df = _prompt_attention(qsb, ksb, vsb, qdf, kdf, vdf, g_sb, g_df, slopes, lam_vecs, lam_init)
    y_prompt, conv_tail = _post(x_prompt, o_sb, o_df, *post_w, tm=PROMPT_TM)
    p_conv = conv_tail[:, 8 - (CONV_W - 1):]

    (s_sbk, s_sbv, s_dfk, s_dfv, sqsb, _, _, sqdf, _, _) = _proj(
        x_sample.reshape(1, nb * t_new, d), g_mix, w_in_b, F32, stored_layout=False, tm=nb * t_new)
    s3 = lambda a: a.reshape(nb, t_new, SEG)
    pos_minor = lambda c: jnp.moveaxis(c.reshape(n_pool, page, SEG), 1, 2)
    so_sb, so_df = _decode_attention(
        page_table, slopes, lam_vecs, s3(sqsb), s3(sqdf), s3(s_sbk), s3(s_sbv), s3(s_dfk), s3(s_dfv),
        g_sb_full, g_df, pos_minor(cache_sb_k), pos_minor(cache_sb_v), pos_minor(cache_df_k),
        cache_df_v.reshape(n_pool, page * df_heads, 2 * HEAD_DIM), lam_init)
    st = state_conv[0]
    zeros = jnp.zeros((nb, t_new - 1, 2 * d_ff), F32)
    prev1 = jnp.concatenate([st[:, 1:2], zeros], axis=1).reshape(nb * t_new, 2 * d_ff)
    prev2 = jnp.concatenate([st, zeros[:, 1:]], axis=1).reshape(nb * t_new, 2 * d_ff)
    y_sample, u_s = _post(x_sample.reshape(nb * t_new, d), so_sb.reshape(nb * t_new, SEG),
                          so_df.reshape(nb * t_new, SEG), *post_w, prev=(prev1, prev2, t_new),
                          tm=nb * t_new)
    s_conv = u_s.reshape(nb, t_new, 2 * d_ff)[:, t_new - (CONV_W - 1):]

    return (y_prompt, y_sample.reshape(nb, t_new, d),
            p_sbk[None], p_sbv[None], p_dfk[None], p_dfv.reshape(1, b, t, df_heads, 2 * HEAD_DIM),
            p_conv[None],
            s_sbk.reshape(1, nb, t_new, sb_heads, HEAD_DIM), s_sbv.reshape(1, nb, t_new, sb_heads, HEAD_DIM),
            s_dfk.reshape(1, nb, t_new, df_heads, 2, HEAD_DIM), s_dfv.reshape(1, nb, t_new, df_heads, 2 * HEAD_DIM),
            s_conv[None])
```

```python
import functools
import math

import numpy as np
import jax
import jax.numpy as jnp
from jax import lax
from jax.experimental import pallas as pl
from jax.experimental.pallas import tpu as pltpu

F32 = jnp.float32
BF16 = jnp.bfloat16
I32 = jnp.int32

EPS = 1e-6
HEAD_DIM = 64
LANES = 128
SEG = 512
N_SEG = 6
CONV_W = 3
NEG_INF = float("-inf")
VMEM_LIMIT = 48 * 1024 * 1024
PROMPT_TM = 512
F32_EXP_ZERO = -104.0


def _rms(x, g):
    ms = jnp.mean(x * x, axis=-1, keepdims=True)
    return x * lax.rsqrt(ms + EPS) * g


def _nt_dot(a, b):
    return lax.dot_general(a, b, (((1,), (1,)), ((), ())), preferred_element_type=F32)


def _dot(a, b):
    return jnp.dot(a, b, preferred_element_type=F32)


def _cumsum_matrix(tk):
    j = lax.broadcasted_iota(I32, (2 * tk, tk), 0)
    s = lax.broadcasted_iota(I32, (2 * tk, tk), 1)
    j = jnp.where(j >= tk, j - tk, j)
    return jnp.where(j >= s, 1.0, 0.0).astype(BF16)


def _rev_cumsum(l, uu):
    hi = l.astype(BF16)
    lo = (l - hi.astype(F32)).astype(BF16)
    return _dot(jnp.concatenate([hi, lo], axis=1), uu)


def _log1m_beta(z):
    return -(jnp.maximum(z, 0.0) + jnp.log(1.0 + jnp.exp(-jnp.abs(z))))


def _proj_kernel(x_ref, g_ref, w_ref, sbk_ref, sbv_ref, dfk_ref, dfv_ref,
                 qsb_ref, ksb_ref, vsb_ref, qdf_ref, kdf_ref, vdf_ref, *, stored_layout):
    h = _rms(x_ref[...], g_ref[...]).astype(BF16)
    tm = h.shape[0]

    def seg(j):
        return _dot(h, w_ref[:, j * SEG:(j + 1) * SEG])

    def put(ref, val):
        ref[...] = val.T if stored_layout else val

    scale = HEAD_DIM ** -0.5
    qsb_ref[...] = (seg(0) * scale).astype(qsb_ref.dtype)
    k = seg(1)
    put(sbk_ref, k)
    ksb_ref[...] = k.astype(BF16)
    v = seg(2)
    put(sbv_ref, v)
    vsb_ref[...] = v.astype(BF16)
    qdf_ref[...] = (seg(3) * scale).astype(qdf_ref.dtype)
    k = seg(4)
    put(dfk_ref, k)
    kdf_ref[...] = k.astype(BF16)
    v = seg(5)
    if stored_layout:
        n_heads = SEG // LANES
        for hd in range(n_heads):
            dfv_ref[pl.ds(hd, tm, stride=n_heads), :] = v[:, hd * LANES:(hd + 1) * LANES]
    else:
        dfv_ref[...] = v
    vdf_ref[...] = v.astype(BF16)


def _proj(x, g, w, q_dtype, stored_layout, tm):
    b, t, d = x.shape
    grid = (b, t // tm)
    rows = pl.BlockSpec((None, tm, SEG), lambda bi, ti: (bi, ti, 0))
    const = lambda bi, ti: (0, 0)
    f32o = jax.ShapeDtypeStruct((b, t, SEG), F32)
    b16o = jax.ShapeDtypeStruct((b, t, SEG), BF16)
    qo = jax.ShapeDtypeStruct((b, t, SEG), q_dtype)
    if stored_layout:
        n_heads = SEG // LANES
        tspec = pl.BlockSpec((None, SEG, tm), lambda bi, ti: (bi, 0, ti))
        to = jax.ShapeDtypeStruct((b, SEG, t), F32)
        kv_specs = [tspec, tspec, tspec, pl.BlockSpec((None, tm * n_heads, LANES), lambda bi, ti: (bi, ti, 0))]
        kv_shapes = [to, to, to, jax.ShapeDtypeStruct((b, t * n_heads, LANES), F32)]
    else:
        kv_specs = [rows] * 4
        kv_shapes = [f32o] * 4
    return pl.pallas_call(
        functools.partial(_proj_kernel, stored_layout=stored_layout),
        grid=grid,
        in_specs=[pl.BlockSpec((None, tm, d), lambda bi, ti: (bi, ti, 0)), pl.BlockSpec((1, d), const),
                  pl.BlockSpec((d, N_SEG * SEG), const, pipeline_mode=pl.Buffered(1))],
        out_specs=kv_specs + [rows] * 6,
        out_shape=kv_shapes + [qo, b16o, b16o, qo, b16o, b16o],
        compiler_params=pltpu.CompilerParams(dimension_semantics=("arbitrary", "arbitrary"),
                                             vmem_limit_bytes=VMEM_LIMIT),
        name="proj",
    )(x, g, w)


def _split_lane_halves_rows(x):
    lo = lax.broadcasted_iota(I32, x.shape, 1) < HEAD_DIM
    zero = jnp.zeros_like(x)
    return jnp.concatenate([jnp.where(lo, x, zero), jnp.where(lo, zero, x)], axis=0)


def _sb_prompt_kernel(q_ref, k_ref, v_ref, g_ref, o_ref, q2_ref, c_ref, acc_ref, *, tq, tk):
    qi = pl.program_id(1)
    n_blk = SEG // LANES
    for p in range(n_blk):
        q2_ref[p] = _split_lane_halves_rows(q_ref[:, p * LANES:(p + 1) * LANES])
    c_ref[...] = jnp.zeros_like(c_ref)
    acc_ref[...] = jnp.zeros_like(acc_ref)
    uu = _cumsum_matrix(tk)
    row = lax.broadcasted_iota(I32, (2 * tq, tk), 0)
    col = lax.broadcasted_iota(I32, (2 * tq, tk), 1)
    qpos = qi * tq + jnp.where(row >= tq, row - tq, row)

    def block(kj, masked):
        start = pl.multiple_of(kj * tk, tk)
        if masked:
            strict = start + col < qpos
        for p in range(n_blk):
            lanes = slice(p * LANES, (p + 1) * LANES)
            z = _nt_dot(q2_ref[p], k_ref[pl.ds(start, tk), lanes])
            l = _log1m_beta(z)
            if masked:
                l = jnp.where(strict, l, 0.0)
            cinc = _rev_cumsum(l, uu)
            c = c_ref[p]
            a = jnp.exp(z + c + cinc)
            if masked:
                a = jnp.where(strict, a, 0.0)
            c_ref[p] = c + cinc[:, 0:1]
            ab = a.astype(BF16)
            a_cat = jnp.concatenate([ab[:tq], ab[tq:]], axis=1)
            acc_ref[p] += _dot(a_cat, _split_lane_halves_rows(v_ref[pl.ds(start, tk), lanes]))

    n_diag = tq // tk
    top = (qi + 1) * n_diag - 1
    for d in range(n_diag):
        block(top - d, True)

    def c_max():
        c = c_ref[0]
        for p in range(1, n_blk):
            c = jnp.maximum(c, c_ref[p])
        return jnp.max(c)

    def body(carry):
        kj, _ = carry
        block(kj, False)
        return kj - 1, c_max()

    lax.while_loop(lambda carry: (carry[0] >= 0) & (carry[1] > F32_EXP_ZERO), body,
                   (top - n_diag, c_max()))

    lo = lax.broadcasted_iota(I32, (tq, LANES), 1) < HEAD_DIM
    for p in range(n_blk):
        o = acc_ref[p]
        o2 = o * o
        s_lo = jnp.sum(jnp.where(lo, o2, 0.0), axis=-1, keepdims=True)
        s_hi = jnp.sum(jnp.where(lo, 0.0, o2), axis=-1, keepdims=True)
        ms = jnp.where(lo, s_lo, s_hi) * (1.0 / HEAD_DIM)
        o_ref[:, p * LANES:(p + 1) * LANES] = (o * lax.rsqrt(ms + EPS) * g_ref[...]).astype(o_ref.dtype)


def _lam_value(lam_ref, lam_init):
    lv = lam_ref[...]
    s1 = jnp.sum(lv[0:1] * lv[1:2], axis=-1, keepdims=True)
    s2 = jnp.sum(lv[2:3] * lv[3:4], axis=-1, keepdims=True)
    return jnp.exp(s1) - jnp.exp(s2) + lam_init


def _df_prompt_kernel(slopes_ref, lam_ref, q_ref, k_ref, v_ref, g_ref, o_ref, q2_ref, m_ref, acc_ref,
                      *, tq, tk, lam_init):
    qi = pl.program_id(1)
    n_heads = SEG // LANES
    for h in range(n_heads):
        q2_ref[h] = _split_lane_halves_rows(q_ref[:, h * LANES:(h + 1) * LANES])
    m_ref[...] = jnp.full_like(m_ref, NEG_INF)
    acc_ref[...] = jnp.zeros_like(acc_ref)
    row = lax.broadcasted_iota(I32, (2 * tq, tk), 0)
    col = lax.broadcasted_iota(I32, (2 * tq, tk), 1)
    rel = jnp.where(row >= tq, row - tq, row) - col
    ones = jnp.ones((tk, LANES), BF16)

    def block(kj, masked):
        start = pl.multiple_of(kj * tk, tk)
        dist_i = rel + (qi * tq - start)
        dist = dist_i.astype(F32)
        for h in range(n_heads):
            lanes = slice(h * LANES, (h + 1) * LANES)
            s = _nt_dot(q2_ref[h], k_ref[pl.ds(start, tk), lanes]) - slopes_ref[h] * dist
            if masked:
                s = jnp.where(dist_i >= 0, s, NEG_INF)
            m_i = m_ref[h]
            m_new = jnp.maximum(m_i, jnp.max(s, axis=-1, keepdims=True))
            alpha = jnp.exp(m_i - m_new)
            p = jnp.exp(s - m_new).astype(BF16)
            v_aug = jnp.concatenate([v_ref[pl.ds(start, tk), lanes], ones], axis=1)
            acc_ref[h] = alpha * acc_ref[h] + _dot(p, v_aug)
            m_ref[h] = m_new

    top = ((qi + 1) * tq - 1) // tk
    block(top, True)
    lax.fori_loop(0, top, lambda i, _: block(top - 1 - i, False), None)

    lam = _lam_value(lam_ref, lam_init)
    for h in range(n_heads):
        acc = acc_ref[h]
        r = acc[:, :LANES] / acc[:, LANES:]
        o = r[:tq] - lam * r[tq:]
        o_ref[:, h * LANES:(h + 1) * LANES] = (_rms(o, g_ref[...]) * (1.0 - lam_init)).astype(o_ref.dtype)


def _prompt_attention(qsb, ksb, vsb, qdf, kdf, vdf, g_sb, g_df, slopes, lam_vecs, lam_init,
                      tq_sb=256, tk_sb=256, tq_df=256, tk_df=1024):
    b, t, _ = qsb.shape
    n_blk = SEG // LANES
    qspec = lambda tq: pl.BlockSpec((None, tq, SEG), lambda bi, qi: (bi, qi, 0))
    kvspec = pl.BlockSpec((None, t, SEG), lambda bi, qi: (bi, 0, 0))
    gspec = pl.BlockSpec((1, LANES), lambda bi, qi: (0, 0))
    out = jax.ShapeDtypeStruct((b, t, SEG), BF16)
    params = pltpu.CompilerParams(dimension_semantics=("arbitrary",) * 2, vmem_limit_bytes=VMEM_LIMIT)
    q2_scratch = lambda tq: pltpu.VMEM((n_blk, 2 * tq, LANES), BF16)
    o_sb = pl.pallas_call(
        functools.partial(_sb_prompt_kernel, tq=tq_sb, tk=tk_sb),
        grid=(b, t // tq_sb), in_specs=[qspec(tq_sb), kvspec, kvspec, gspec], out_specs=qspec(tq_sb),
        out_shape=out,
        scratch_shapes=[q2_scratch(tq_sb), pltpu.VMEM((n_blk, 2 * tq_sb, 1), F32),
                        pltpu.VMEM((n_blk, tq_sb, LANES), F32)],
        compiler_params=params, name="sb_prompt",
    )(qsb, ksb, vsb, g_sb)
    o_df = pl.pallas_call(
        functools.partial(_df_prompt_kernel, tq=tq_df, tk=tk_df, lam_init=lam_init),
        grid=(b, t // tq_df),
        in_specs=[pl.BlockSpec(memory_space=pltpu.SMEM),
                  pl.BlockSpec((4, HEAD_DIM), lambda bi, qi: (0, 0)),
                  qspec(tq_df), kvspec, kvspec, gspec],
        out_specs=qspec(tq_df), out_shape=out,
        scratch_shapes=[q2_scratch(tq_df), pltpu.VMEM((n_blk, 2 * tq_df, 1), F32),
                        pltpu.VMEM((n_blk, 2 * tq_df, 2 * LANES), F32)],
        compiler_params=params, name="df_prompt",
    )(slopes, lam_vecs, qdf, kdf, vdf, g_df)
    return o_sb, o_df


def _replicated_queries(q, t_new):
    n_rows = SEG // HEAD_DIM * t_new
    lane_chunk = lax.broadcasted_iota(I32, (n_rows, SEG), 1) // HEAD_DIM
    own = lane_chunk == lax.broadcasted_iota(I32, (n_rows, SEG), 0) // t_new
    return jnp.where(own, jnp.concatenate([q] * (n_rows // t_new), axis=0), 0.0).astype(BF16)


def _pad_rows(x, page):
    return jnp.concatenate([x, jnp.zeros((page - x.shape[0], SEG), F32)], axis=0).astype(BF16)


def _sb_pages(qs, k_refs, v_refs, c, acc_t, uu):
    n_rows, page = qs.shape[0], uu.shape[1]
    z = [_dot(qs, k[...].astype(BF16)) for k in k_refs]
    cinc = _rev_cumsum(jnp.concatenate([_log1m_beta(zu) for zu in z], axis=0), uu)
    pad = jnp.zeros((LANES - n_rows, page), BF16)
    for u, v in enumerate(v_refs):
        cu = cinc[u * n_rows:(u + 1) * n_rows]
        a = jnp.exp(z[u] + c + cu).astype(BF16)
        c = c + cu[:, 0:1]
        acc_t = acc_t + _nt_dot(v[...].astype(BF16), jnp.concatenate([a, pad], axis=0))
    return c, acc_t


def _sb_decode_top_kernel(pt_ref, q_ref, nk_ref, nv_ref, *rest, pps, page, t_new):
    ck, cv = rest[:pps], rest[pps:2 * pps]
    c_out, acc_out, acct_out = rest[2 * pps:]
    n_rows = SEG // HEAD_DIM * t_new
    uu = _cumsum_matrix(page)
    qs = _replicated_queries(q_ref[...], t_new)
    i_q = lax.broadcasted_iota(I32, (n_rows, 1), 0) % t_new
    strict = lax.broadcasted_iota(I32, (n_rows, page), 1) < i_q
    z = _nt_dot(qs, _pad_rows(nk_ref[...], page))
    cinc = _rev_cumsum(jnp.where(strict, _log1m_beta(z), 0.0), uu)
    a = jnp.where(strict, jnp.exp(z + cinc), 0.0).astype(BF16)
    acc_out[...] = _dot(a, _pad_rows(nv_ref[...], page))
    c, acc_t = _sb_pages(qs, ck, cv, cinc[:, 0:1], jnp.zeros((SEG, LANES), F32), uu)
    c_out[...] = jnp.broadcast_to(c, (n_rows, LANES))
    acct_out[...] = acc_t


def _sb_decode_rest_kernel(pt_ref, live_ref, q_ref, c_in, acc_in, acct_in, g_ref, *rest, pps, page, t_new):
    ck, cv = rest[:pps], rest[pps:2 * pps]
    o_ref, qr_ref, c_ref, acct_ref = rest[2 * pps:]
    b, j = pl.program_id(0), pl.program_id(1)
    n_rows = SEG // HEAD_DIM * t_new

    @pl.when(j == 0)
    def _():
        qr_ref[...] = _replicated_queries(q_ref[...], t_new)
        c_ref[...] = c_in[:, 0:1]
        acct_ref[...] = acct_in[...]

    @pl.when(live_ref[b] == 1)
    def _():
        c, acc_t = _sb_pages(qr_ref[...], ck, cv, c_ref[...], acct_ref[...], _cumsum_matrix(page))
        c_ref[...] = c
        acct_ref[...] = acc_t

    @pl.when(j == pl.num_programs(1) - 1)
    def _():
        lane = lax.broadcasted_iota(I32, (t_new, SEG), 1)
        acc = acc_in[...] + acct_ref[...].T[:n_rows]
        o = jnp.zeros((t_new, SEG), F32)
        for h in range(SEG // HEAD_DIM):
            o = o + jnp.where(lane // HEAD_DIM == h, acc[h * t_new:(h + 1) * t_new], 0.0)
        o2 = o * o
        ms = jnp.zeros((t_new, SEG), F32)
        for h in range(SEG // HEAD_DIM):
            sel = lane // HEAD_DIM == h
            ms = jnp.where(sel, jnp.sum(jnp.where(sel, o2, 0.0), axis=-1, keepdims=True), ms)
        o_ref[...] = o * lax.rsqrt(ms * (1.0 / HEAD_DIM) + EPS) * g_ref[...]


def _df_decode_kernel(pt_ref, slopes_ref, lam_ref, q_ref, nk_ref, nv_ref, g_ref, *rest,
                      n_pages, pps, page, t_new, lam_init):
    ck, cv = rest[:pps], rest[pps:2 * pps]
    o_ref, qr_ref, m_ref, l_ref, acc_ref = rest[2 * pps:]
    j = pl.program_id(1)
    n_heads = SEG // LANES
    n_rows = SEG // HEAD_DIM * t_new
    past_len = n_pages * page
    rowq = lax.broadcasted_iota(I32, (n_rows, 1), 0)
    i_q = rowq % t_new
    head_of_row = rowq // (2 * t_new)
    slope = jnp.zeros((n_rows, 1), F32)
    for h in range(n_heads):
        slope = jnp.where(head_of_row == h, slopes_ref[h], slope)
    col = lax.broadcasted_iota(I32, (n_rows, page), 1)

    @pl.when(j == 0)
    def _():
        qr_ref[...] = _replicated_queries(q_ref[...], t_new)
        s = _nt_dot(qr_ref[...], _pad_rows(nk_ref[...], page)) - slope * (i_q - col).astype(F32)
        s = jnp.where(col <= i_q, s, NEG_INF)
        m_new = jnp.max(s, axis=-1, keepdims=True)
        p = jnp.exp(s - m_new)
        m_ref[...] = m_new
        l_ref[...] = jnp.sum(p, axis=-1, keepdims=True)
        pv = _dot(p.astype(BF16), _pad_rows(nv_ref[...], page))
        acc = jnp.zeros((n_rows, LANES), F32)
        for h in range(n_heads):
            acc = jnp.where(head_of_row == h, pv[:, h * LANES:(h + 1) * LANES], acc)
        acc_ref[...] = acc

    qd = qr_ref[...]
    s = []
    for u in range(pps):
        lp = n_pages - 1 - (j * pps + u)
        dist = (past_len + i_q - lp * page - col).astype(F32)
        s.append(_dot(qd, ck[u][...].astype(BF16)) - slope * dist)
    smax = s[0]
    for u in range(1, pps):
        smax = jnp.maximum(smax, s[u])
    m_i = m_ref[...]
    m_new = jnp.maximum(m_i, jnp.max(smax, axis=-1, keepdims=True))
    alpha = jnp.exp(m_i - m_new)
    psum = jnp.zeros((n_rows, page), F32)
    acc = alpha * acc_ref[...]
    for u in range(pps):
        p = jnp.exp(s[u] - m_new)
        psum = psum + p
        p_bd = jnp.concatenate([jnp.where(head_of_row == h, p, 0.0) for h in range(n_heads)], axis=1)
        v_hm = jnp.concatenate([cv[u][pl.ds(h, page, stride=n_heads), :] for h in range(n_heads)], axis=0)
        acc = acc + _dot(p_bd.astype(BF16), v_hm.astype(BF16))
    l_ref[...] = alpha * l_ref[...] + jnp.sum(psum, axis=-1, keepdims=True)
    acc_ref[...] = acc
    m_ref[...] = m_new

    @pl.when(j == pl.num_programs(1) - 1)
    def _():
        lam = _lam_value(lam_ref, lam_init)
        acc = acc_ref[...]
        l_i = l_ref[...]
        outs = []
        for h in range(n_heads):
            r0 = 2 * h * t_new
            o0 = acc[r0:r0 + t_new] / l_i[r0:r0 + t_new]
            o1 = acc[r0 + t_new:r0 + 2 * t_new] / l_i[r0 + t_new:r0 + 2 * t_new]
            outs.append(_rms(o0 - lam * o1, g_ref[...]) * (1.0 - lam_init))
        o_ref[...] = jnp.concatenate(outs, axis=1)


def _decode_attention(page_table, slopes, lam_vecs, qsb, qdf, nsbk, nsbv, ndfk, ndfv, g_sb, g_df,
                      c_sbk, c_sbv, c_dfk, c_dfv, lam_init, pages_per_step=16, rest_pages_per_step=28):
    nb, t_new, _ = qsb.shape
    n_pages = page_table.shape[1]
    page = c_sbk.shape[2]
    pps = pages_per_step
    n_rows = SEG // HEAD_DIM * t_new
    slab = (None, SEG, page)
    o = jax.ShapeDtypeStruct((nb, t_new, SEG), F32)
    params = lambda n: pltpu.CompilerParams(dimension_semantics=("arbitrary",) * n, vmem_limit_bytes=VMEM_LIMIT)

    per_b = pl.BlockSpec((None, t_new, SEG), lambda b, j, pt: (b, 0, 0))
    const2 = lambda shape: pl.BlockSpec(shape, lambda b, j, pt: (0, 0))
    df_pages = [pl.BlockSpec(slab, functools.partial(
        lambda b, j, pt, u: (pt[b, n_pages - 1 - (j * pps + u)], 0, 0), u=u)) for u in range(pps)]
    o_df = pl.pallas_call(
        functools.partial(_df_decode_kernel, n_pages=n_pages, pps=pps, page=page, t_new=t_new,
                          lam_init=lam_init),
        grid_spec=pltpu.PrefetchScalarGridSpec(
            num_scalar_prefetch=1, grid=(nb, n_pages // pps),
            in_specs=[pl.BlockSpec(memory_space=pltpu.SMEM), const2((4, HEAD_DIM)), per_b, per_b, per_b,
                      const2((1, LANES))] + df_pages * 2,
            out_specs=per_b,
            scratch_shapes=[pltpu.VMEM((n_rows, SEG), BF16), pltpu.VMEM((n_rows, 1), F32),
                            pltpu.VMEM((n_rows, 1), F32), pltpu.VMEM((n_rows, LANES), F32)]),
        out_shape=o, compiler_params=params(2), name="df_decode",
    )(page_table, slopes, lam_vecs, qdf, ndfk, ndfv, g_df, *([c_dfk] * pps + [c_dfv] * pps))

    row_b = lambda shape: pl.BlockSpec((None,) + shape, lambda b, pt: (b, 0, 0))
    top_pages = [pl.BlockSpec(slab, functools.partial(
        lambda b, pt, u: (pt[b, n_pages - 1 - u], 0, 0), u=u)) for u in range(pps)]
    state_shapes = [jax.ShapeDtypeStruct((nb, n_rows, LANES), F32), jax.ShapeDtypeStruct((nb, n_rows, SEG), F32),
                    jax.ShapeDtypeStruct((nb, SEG, LANES), F32)]
    c_top, acc_new, acc_t = pl.pallas_call(
        functools.partial(_sb_decode_top_kernel, pps=pps, page=page, t_new=t_new),
        grid_spec=pltpu.PrefetchScalarGridSpec(
            num_scalar_prefetch=1, grid=(nb,),
            in_specs=[row_b((t_new, SEG))] * 3 + top_pages * 2,
            out_specs=[row_b((n_rows, LANES)), row_b((n_rows, SEG)), row_b((SEG, LANES))]),
        out_shape=state_shapes, compiler_params=params(1), name="sb_decode_top",
    )(page_table, qsb, nsbk, nsbv, *([c_sbk] * pps + [c_sbv] * pps))

    live = (jnp.max(c_top, axis=(1, 2)) > F32_EXP_ZERO).astype(I32)
    n_rest = n_pages - pps
    pps = max(p for p in range(1, rest_pages_per_step + 1) if n_rest % p == 0)

    def rest_index(b, j, pt, live, u):
        lp = n_rest - 1 - (j * pps + u)
        return (jnp.where(live[b] == 1, pt[b, lp], pt[0, 0]), 0, 0)

    rest_pages = [pl.BlockSpec(slab, functools.partial(rest_index, u=u)) for u in range(pps)]
    row_bj = lambda shape: pl.BlockSpec((None,) + shape, lambda b, j, pt, live: (b, 0, 0))
    o_sb = pl.pallas_call(
        functools.partial(_sb_decode_rest_kernel, pps=pps, page=page, t_new=t_new),
        grid_spec=pltpu.PrefetchScalarGridSpec(
            num_scalar_prefetch=2, grid=(nb, n_rest // pps),
            in_specs=[row_bj((t_new, SEG)), row_bj((n_rows, LANES)), row_bj((n_rows, SEG)),
                      row_bj((SEG, LANES)), pl.BlockSpec((1, SEG), lambda b, j, pt, live: (0, 0))]
                     + rest_pages * 2,
            out_specs=row_bj((t_new, SEG)),
            scratch_shapes=[pltpu.VMEM((n_rows, SEG), BF16), pltpu.VMEM((n_rows, 1), F32),
                            pltpu.VMEM((SEG, LANES), F32)]),
        out_shape=o, compiler_params=params(2), name="sb_decode_rest",
    )(page_table, live, qsb, c_top, acc_new, acc_t, g_sb, *([c_sbk] * pps + [c_sbv] * pps))
    return o_sb, o_df


def _gelu_tanh(x):
    return 0.5 * x * (1.0 + jnp.tanh(math.sqrt(2.0 / math.pi) * (x + 0.044715 * x * x * x)))


def _post_kernel(x_ref, msb_ref, mdf_ref, woa_ref, wob_ref, gffn_ref, wup_ref, cw_ref, cb_ref, wdn_ref,
                 gfin_ref, *rest, d_ff, fc, seq_rows):
    if seq_rows is None:
        y_ref, conv_ref, carry_ref = rest
    else:
        p1_ref, p2_ref, y_ref, conv_ref = rest
    tm = x_ref.shape[0]
    h1 = x_ref[...] + _dot(msb_ref[...].astype(BF16), woa_ref[...]) + _dot(mdf_ref[...].astype(BF16), wob_ref[...])
    h2 = _rms(h1, gffn_ref[...]).astype(BF16)
    rowi = lax.broadcasted_iota(I32, (tm, fc), 0)

    if seq_rows is None:
        @pl.when(pl.program_id(1) == 0)
        def _():
            carry_ref[...] = jnp.zeros_like(carry_ref)

    def conv(cols):
        u = _dot(h2, wup_ref[:, cols])
        r1 = pltpu.roll(u, 1, axis=0)
        r2 = pltpu.roll(u, 2, axis=0)
        if seq_rows is None:
            prev = carry_ref[:, cols]
            r1 = jnp.where(rowi == 0, prev[7:8], r1)
            r2 = jnp.where(rowi == 0, prev[6:7], jnp.where(rowi == 1, prev[7:8], r2))
            carry_ref[:, cols] = u[tm - 8:]
            conv_ref[:, cols] = u[tm - 8:]
        else:
            pos = rowi % seq_rows
            r1 = jnp.where(pos == 0, p1_ref[:, cols], r1)
            r2 = jnp.where(pos < 2, p2_ref[:, cols], r2)
            conv_ref[:, cols] = u
        cw = cw_ref[:, cols]
        return cb_ref[:, cols] + cw[2:3] * u + cw[1:2] * r1 + cw[0:1] * r2

    acc = jnp.zeros((tm, x_ref.shape[1]), F32)
    for c in range(d_ff // fc):
        gate = conv(slice(c * fc, (c + 1) * fc))
        val = conv(slice(d_ff + c * fc, d_ff + (c + 1) * fc))
        act = (_gelu_tanh(gate) * val).astype(BF16)
        acc = acc + _dot(act, wdn_ref[c * fc:(c + 1) * fc, :])
    y_ref[...] = _rms(h1 + acc, gfin_ref[...])


def _post(x, msb, mdf, woa, wob, gffn, wup, cw, cb, wdn, gfin, prev=None, *, tm, fc=256):
    d = x.shape[-1]
    d_ff = wdn.shape[0]
    weights = (woa, wob, gffn, wup, cw, cb, wdn, gfin)
    if prev is None:
        b, t, _ = x.shape
        grid = (b, t // tm)
        act = lambda w: pl.BlockSpec((None, tm, w), lambda bi, ti: (bi, ti, 0))
        const = lambda a: pl.BlockSpec(a.shape, lambda bi, ti: (0,) * a.ndim, pipeline_mode=pl.Buffered(1))
        in_specs = [act(d), act(SEG), act(SEG)] + [const(a) for a in weights]
        out_specs = [act(d), pl.BlockSpec((None, 8, 2 * d_ff), lambda bi, ti: (bi, 0, 0))]
        out_shape = [jax.ShapeDtypeStruct((b, t, d), F32), jax.ShapeDtypeStruct((b, 8, 2 * d_ff), F32)]
        scratch = [pltpu.VMEM((8, 2 * d_ff), F32)]
        args = (x, msb, mdf) + weights
        kern = functools.partial(_post_kernel, d_ff=d_ff, fc=fc, seq_rows=None)
        name = "post_prompt"
    else:
        p1, p2, seq_rows = prev
        m = x.shape[0]
        grid = (m // tm, 1)
        act = lambda w: pl.BlockSpec((tm, w), lambda mi, ti: (mi, 0))
        const = lambda a: pl.BlockSpec(a.shape, lambda mi, ti: (0,) * a.ndim)
        in_specs = ([act(d), act(SEG), act(SEG)] + [const(a) for a in weights]
                    + [act(2 * d_ff), act(2 * d_ff)])
        out_specs = [act(d), act(2 * d_ff)]
        out_shape = [jax.ShapeDtypeStruct((m, d), F32), jax.ShapeDtypeStruct((m, 2 * d_ff), F32)]
        scratch = []
        args = (x, msb, mdf) + weights + (p1, p2)
        kern = functools.partial(_post_kernel, d_ff=d_ff, fc=fc, seq_rows=seq_rows)
        name = "post_sample"
    return pl.pallas_call(
        kern, grid=grid, in_specs=in_specs, out_specs=out_specs, out_shape=out_shape,
        scratch_shapes=scratch,
        compiler_params=pltpu.CompilerParams(dimension_semantics=("arbitrary", "arbitrary"),
                                             vmem_limit_bytes=56 * 1024 * 1024),
        name=name,
    )(*args)


def kernel(x_prompt, x_sample, cache_sb_k, cache_sb_v, cache_df_k, cache_df_v, state_conv, page_table,
           norm_mix, w_in, sb_head_norm, df_head_norm, lam_q1, lam_k1, lam_q2, lam_k2, w_o,
           norm_ffn, w_up, conv_w, conv_b, w_down, norm_final):
    depth = w_in.shape[0]
    assert depth == 1, "single-layer step"
    b, t, d = x_prompt.shape
    nb, t_new, _ = x_sample.shape
    n_pool, page = cache_sb_k.shape[1], cache_sb_k.shape[2]
    d_ff = w_down.shape[1]
    sb_heads = cache_sb_k.shape[3]
    df_heads = cache_df_k.shape[3]
    lam_init = 0.8 - 0.6 * math.exp(-0.3 * 0)
    slopes = jnp.asarray(2.0 ** (-8.0 * np.arange(1, df_heads + 1) / df_heads), dtype=F32)

    g_mix = norm_mix[0].reshape(1, d)
    w_in_b = w_in[0].astype(BF16)
    g_sb = jnp.tile(sb_head_norm[0], LANES // HEAD_DIM).reshape(1, LANES)
    g_sb_full = jnp.tile(sb_head_norm[0], SEG // HEAD_DIM).reshape(1, SEG)
    g_df = df_head_norm[0].reshape(1, LANES)
    lam_vecs = jnp.stack([lam_q1[0], lam_k1[0], lam_q2[0], lam_k2[0]]).astype(F32)
    wo_b = w_o[0].astype(BF16)
    post_w = (wo_b[:SEG], wo_b[SEG:], norm_ffn[0].reshape(1, d), w_up[0].astype(BF16), conv_w[0],
              conv_b[0].reshape(1, 2 * d_ff), w_down[0].astype(BF16), norm_final.reshape(1, d))

    (p_sbk, p_sbv, p_dfk, p_dfv, qsb, ksb, vsb, qdf, kdf, vdf) = _proj(
        x_prompt, g_mix, w_in_b, BF16, stored_layout=True, tm=PROMPT_TM)
    p_sbk = jnp.moveaxis(p_sbk.reshape(b, sb_heads, HEAD_DIM, t), 3, 1)
    p_sbv = jnp.moveaxis(p_sbv.reshape(b, sb_heads, HEAD_DIM, t), 3, 1)
    p_dfk = jnp.moveaxis(p_dfk.reshape(b, df_heads, 2, HEAD_DIM, t), 4, 1)
    o_sb, o_df = _prompt_attention(qsb, ksb, vsb, qdf, kdf, vdf, g_sb, g_df, slopes, lam_vecs, lam_init)
    y_prompt, conv_tail = _post(x_prompt, o_sb, o_df, *post_w, tm=PROMPT_TM)
    p_conv = conv_tail[:, 8 - (CONV_W - 1):]

    (s_sbk, s_sbv, s_dfk, s_dfv, sqsb, _, _, sqdf, _, _) = _proj(
        x_sample.reshape(1, nb * t_new, d), g_mix, w_in_b, F32, stored_layout=False, tm=nb * t_new)
    s3 = lambda a: a.reshape(nb, t_new, SEG)
    pos_minor = lambda c: jnp.moveaxis(c.reshape(n_pool, page, SEG), 1, 2)
    so_sb, so_df = _decode_attention(
        page_table, slopes, lam_vecs, s3(sqsb), s3(sqdf), s3(s_sbk), s3(s_sbv), s3(s_dfk), s3(s_dfv),
        g_sb_full, g_df, pos_minor(cache_sb_k), pos_minor(cache_sb_v), pos_minor(cache_df_k),
        cache_df_v.reshape(n_pool, page * df_heads, 2 * HEAD_DIM), lam_init)
    st = state_conv[0]
    zeros = jnp.zeros((nb, t_new - 1, 2 * d_ff), F32)
    prev1 = jnp.concatenate([st[:, 1:2], zeros], axis=1).reshape(nb * t_new, 2 * d_ff)
    prev2 = jnp.concatenate([st, zeros[:, 1:]], axis=1).reshape(nb * t_new, 2 * d_ff)
    y_sample, u_s = _post(x_sample.reshape(nb * t_new, d), so_sb.reshape(nb * t_new, SEG),
                          so_df.reshape(nb * t_new, SEG), *post_w, prev=(prev1, prev2, t_new),
                          tm=nb * t_new)
    s_conv = u_s.reshape(nb, t_new, 2 * d_ff)[:, t_new - (CONV_W - 1):]

    return (y_prompt, y_sample.reshape(nb, t_new, d),
            p_sbk[None], p_sbv[None], p_dfk[None], p_dfv.reshape(1, b, t, df_heads, 2 * HEAD_DIM),
            p_conv[None],
            s_sbk.reshape(1, nb, t_new, sb_heads, HEAD_DIM), s_sbv.reshape(1, nb, t_new, sb_heads, HEAD_DIM),
            s_dfk.reshape(1, nb, t_new, df_heads, 2, HEAD_DIM), s_dfv.reshape(1, nb, t_new, df_heads, 2 * HEAD_DIM),
            s_conv[None])
```

```python
import functools
import math

import numpy as np
import jax
import jax.numpy as jnp
from jax import lax
from jax.experimental import pallas as pl
from jax.experimental.pallas import tpu as pltpu

F32 = jnp.float32
BF16 = jnp.bfloat16
I32 = jnp.int32

EPS = 1e-6
HEAD_DIM = 64
LANES = 128
SEG = 512
N_SEG = 6
CONV_W = 3
NEG_INF = float("-inf")
VMEM_LIMIT = 48 * 1024 * 1024
PROMPT_TM = 512
F32_EXP_ZERO = -104.0


def _rms(x, g):
    ms = jnp.mean(x * x, axis=-1, keepdims=True)
    return x * lax.rsqrt(ms + EPS) * g


def _nt_dot(a, b):
    return lax.dot_general(a, b, (((1,), (1,)), ((), ())), preferred_element_type=F32)


def _dot(a, b):
    return jnp.dot(a, b, preferred_element_type=F32)


def _cumsum_matrix(tk):
    j = lax.broadcasted_iota(I32, (2 * tk, tk), 0)
    s = lax.broadcasted_iota(I32, (2 * tk, tk), 1)
    j = jnp.where(j >= tk, j - tk, j)
    return jnp.where(j >= s, 1.0, 0.0).astype(BF16)


def _rev_cumsum(l, uu):
    hi = l.astype(BF16)
    lo = (l - hi.astype(F32)).astype(BF16)
    return _dot(jnp.concatenate([hi, lo], axis=1), uu)


def _log1m_beta(z):
    return -(jnp.maximum(z, 0.0) + jnp.log(1.0 + jnp.exp(-jnp.abs(z))))


def _proj_kernel(x_ref, g_ref, w_ref, sbk_ref, sbv_ref, dfk_ref, dfv_ref,
                 qsb_ref, ksb_ref, vsb_ref, qdf_ref, kdf_ref, vdf_ref, *, stored_layout):
    h = _rms(x_ref[...], g_ref[...]).astype(BF16)
    tm = h.shape[0]

    def seg(j):
        return _dot(h, w_ref[:, j * SEG:(j + 1) * SEG])

    def put(ref, val):
        ref[...] = val.T if stored_layout else val

    scale = HEAD_DIM ** -0.5
    qsb_ref[...] = (seg(0) * scale).astype(qsb_ref.dtype)
    k = seg(1)
    put(sbk_ref, k)
    ksb_ref[...] = k.astype(BF16)
    v = seg(2)
    put(sbv_ref, v)
    vsb_ref[...] = v.astype(BF16)
    qdf_ref[...] = (seg(3) * scale).astype(qdf_ref.dtype)
    k = seg(4)
    put(dfk_ref, k)
    kdf_ref[...] = k.astype(BF16)
    v = seg(5)
    if stored_layout:
        n_heads = SEG // LANES
        for hd in range(n_heads):
            dfv_ref[pl.ds(hd, tm, stride=n_heads), :] = v[:, hd * LANES:(hd + 1) * LANES]
    else:
        dfv_ref[...] = v
    vdf_ref[...] = v.astype(BF16)


def _proj(x, g, w, q_dtype, stored_layout, tm):
    b, t, d = x.shape
    grid = (b, t // tm)
    rows = pl.BlockSpec((None, tm, SEG), lambda bi, ti: (bi, ti, 0))
    const = lambda bi, ti: (0, 0)
    f32o = jax.ShapeDtypeStruct((b, t, SEG), F32)
    b16o = jax.ShapeDtypeStruct((b, t, SEG), BF16)
    qo = jax.ShapeDtypeStruct((b, t, SEG), q_dtype)
    if stored_layout:
        n_heads = SEG // LANES
        tspec = pl.BlockSpec((None, SEG, tm), lambda bi, ti: (bi, 0, ti))
        to = jax.ShapeDtypeStruct((b, SEG, t), F32)
        kv_specs = [tspec, tspec, tspec, pl.BlockSpec((None, tm * n_heads, LANES), lambda bi, ti: (bi, ti, 0))]
        kv_shapes = [to, to, to, jax.ShapeDtypeStruct((b, t * n_heads, LANES), F32)]
    else:
        kv_specs = [rows] * 4
        kv_shapes = [f32o] * 4
    return pl.pallas_call(
        functools.partial(_proj_kernel, stored_layout=stored_layout),
        grid=grid,
        in_specs=[pl.BlockSpec((None, tm, d), lambda bi, ti: (bi, ti, 0)), pl.BlockSpec((1, d), const),
                  pl.BlockSpec((d, N_SEG * SEG), const, pipeline_mode=pl.Buffered(1))],
        out_specs=kv_specs + [rows] * 6,
        out_shape=kv_shapes + [qo, b16o, b16o, qo, b16o, b16o],
        compiler_params=pltpu.CompilerParams(dimension_semantics=("arbitrary", "arbitrary"),
                                             vmem_limit_bytes=VMEM_LIMIT),
        name="proj",
    )(x, g, w)


def _split_lane_halves_rows(x):
    lo = lax.broadcasted_iota(I32, x.shape, 1) < HEAD_DIM
    zero = jnp.zeros_like(x)
    return jnp.concatenate([jnp.where(lo, x, zero), jnp.where(lo, zero, x)], axis=0)


def _sb_prompt_kernel(q_ref, k_ref, v_ref, g_ref, o_ref, q2_ref, c_ref, acc_ref, *, tq, tk):
    qi = pl.program_id(1)
    n_blk = SEG // LANES
    for p in range(n_blk):
        q2_ref[p] = _split_lane_halves_rows(q_ref[:, p * LANES:(p + 1) * LANES])
    c_ref[...] = jnp.zeros_like(c_ref)
    acc_ref[...] = jnp.zeros_like(acc_ref)
    uu = _cumsum_matrix(tk)
    row = lax.broadcasted_iota(I32, (2 * tq, tk), 0)
    col = lax.broadcasted_iota(I32, (2 * tq, tk), 1)
    qpos = qi * tq + jnp.where(row >= tq, row - tq, row)

    def block(kj, masked):
        start = pl.multiple_of(kj * tk, tk)
        if masked:
            strict = start + col < qpos
        for p in range(n_blk):
            lanes = slice(p * LANES, (p + 1) * LANES)
            z = _nt_dot(q2_ref[p], k_ref[pl.ds(start, tk), lanes])
            l = _log1m_beta(z)
            if masked:
                l = jnp.where(strict, l, 0.0)
            cinc = _rev_cumsum(l, uu)
            c = c_ref[p]
            a = jnp.exp(z + c + cinc)
            if masked:
                a = jnp.where(strict, a, 0.0)
            c_ref[p] = c + cinc[:, 0:1]
            ab = a.astype(BF16)
            a_cat = jnp.concatenate([ab[:tq], ab[tq:]], axis=1)
            acc_ref[p] += _dot(a_cat, _split_lane_halves_rows(v_ref[pl.ds(start, tk), lanes]))

    n_diag = tq // tk
    top = (qi + 1) * n_diag - 1
    for d in range(n_diag):
        block(top - d, True)

    def c_max():
        c = c_ref[0]
        for p in range(1, n_blk):
            c = jnp.maximum(c, c_ref[p])
        return jnp.max(c)

    def body(carry):
        kj, _ = carry
        block(kj, False)
        return kj - 1, c_max()

    lax.while_loop(lambda carry: (carry[0] >= 0) & (carry[1] > F32_EXP_ZERO), body,
                   (top - n_diag, c_max()))

    lo = lax.broadcasted_iota(I32, (tq, LANES), 1) < HEAD_DIM
    for p in range(n_blk):
        o = acc_ref[p]
        o2 = o * o
        s_lo = jnp.sum(jnp.where(lo, o2, 0.0), axis=-1, keepdims=True)
        s_hi = jnp.sum(jnp.where(lo, 0.0, o2), axis=-1, keepdims=True)
        ms = jnp.where(lo, s_lo, s_hi) * (1.0 / HEAD_DIM)
        o_ref[:, p * LANES:(p + 1) * LANES] = (o * lax.rsqrt(ms + EPS) * g_ref[...]).astype(o_ref.dtype)


def _lam_value(lam_ref, lam_init):
    lv = lam_ref[...]
    s1 = jnp.sum(lv[0:1] * lv[1:2], axis=-1, keepdims=True)
    s2 = jnp.sum(lv[2:3] * lv[3:4], axis=-1, keepdims=True)
    return jnp.exp(s1) - jnp.exp(s2) + lam_init


def _df_prompt_kernel(slopes_ref, lam_ref, q_ref, k_ref, v_ref, g_ref, o_ref, q2_ref, m_ref, acc_ref,
                      *, tq, tk, lam_init):
    qi = pl.program_id(1)
    n_heads = SEG // LANES
    for h in range(n_heads):
        q2_ref[h] = _split_lane_halves_rows(q_ref[:, h * LANES:(h + 1) * LANES])
    m_ref[...] = jnp.full_like(m_ref, NEG_INF)
    acc_ref[...] = jnp.zeros_like(acc_ref)
    row = lax.broadcasted_iota(I32, (2 * tq, tk), 0)
    col = lax.broadcasted_iota(I32, (2 * tq, tk), 1)
    rel = jnp.where(row >= tq, row - tq, row) - col
    ones = jnp.ones((tk, LANES), BF16)

    def block(kj, masked):
        start = pl.multiple_of(kj * tk, tk)
        dist_i = rel + (qi * tq - start)
        dist = dist_i.astype(F32)
        for h in range(n_heads):
            lanes = slice(h * LANES, (h + 1) * LANES)
            s = _nt_dot(q2_ref[h], k_ref[pl.ds(start, tk), lanes]) - slopes_ref[h] * dist
            if masked:
                s = jnp.where(dist_i >= 0, s, NEG_INF)
            m_i = m_ref[h]
            m_new = jnp.maximum(m_i, jnp.max(s, axis=-1, keepdims=True))
            alpha = jnp.exp(m_i - m_new)
            p = jnp.exp(s - m_new).astype(BF16)
            v_aug = jnp.concatenate([v_ref[pl.ds(start, tk), lanes], ones], axis=1)
            acc_ref[h] = alpha * acc_ref[h] + _dot(p, v_aug)
            m_ref[h] = m_new

    top = ((qi + 1) * tq - 1) // tk
    block(top, True)
    lax.fori_loop(0, top, lambda i, _: block(top - 1 - i, False), None)

    lam = _lam_value(lam_ref, lam_init)
    for h in range(n_heads):
        acc = acc_ref[h]
        r = acc[:, :LANES] / acc[:, LANES:]
        o = r[:tq] - lam * r[tq:]
        o_ref[:, h * LANES:(h + 1) * LANES] = (_rms(o, g_ref[...]) * (1.0 - lam_init)).astype(o_ref.dtype)


def _prompt_attention(qsb, ksb, vsb, qdf, kdf, vdf, g_sb, g_df, slopes, lam_vecs, lam_init,
                      tq_sb=256, tk_sb=256, tq_df=256, tk_df=1024):
    b, t, _ = qsb.shape
    n_blk = SEG // LANES
    qspec = lambda tq: pl.BlockSpec((None, tq, SEG), lambda bi, qi: (bi, qi, 0))
    kvspec = pl.BlockSpec((None, t, SEG), lambda bi, qi: (bi, 0, 0))
    gspec = pl.BlockSpec((1, LANES), lambda bi, qi: (0, 0))
    out = jax.ShapeDtypeStruct((b, t, SEG), BF16)
    params = pltpu.CompilerParams(dimension_semantics=("arbitrary",) * 2, vmem_limit_bytes=VMEM_LIMIT)
    q2_scratch = lambda tq: pltpu.VMEM((n_blk, 2 * tq, LANES), BF16)
    o_sb = pl.pallas_call(
        functools.partial(_sb_prompt_kernel, tq=tq_sb, tk=tk_sb),
        grid=(b, t // tq_sb), in_specs=[qspec(tq_sb), kvspec, kvspec, gspec], out_specs=qspec(tq_sb),
        out_shape=out,
        scratch_shapes=[q2_scratch(tq_sb), pltpu.VMEM((n_blk, 2 * tq_sb, 1), F32),
                        pltpu.VMEM((n_blk, tq_sb, LANES), F32)],
        compiler_params=params, name="sb_prompt",
    )(qsb, ksb, vsb, g_sb)
    o_df = pl.pallas_call(
        functools.partial(_df_prompt_kernel, tq=tq_df, tk=tk_df, lam_init=lam_init),
        grid=(b, t // tq_df),
        in_specs=[pl.BlockSpec(memory_space=pltpu.SMEM),
                  pl.BlockSpec((4, HEAD_DIM), lambda bi, qi: (0, 0)),
                  qspec(tq_df), kvspec, kvspec, gspec],
        out_specs=qspec(tq_df), out_shape=out,
        scratch_shapes=[q2_scratch(tq_df), pltpu.VMEM((n_blk, 2 * tq_df, 1), F32),
                        pltpu.VMEM((n_blk, 2 * tq_df, 2 * LANES), F32)],
        compiler_params=params, name="df_prompt",
    )(slopes, lam_vecs, qdf, kdf, vdf, g_df)
    return o_sb, o_df


def _replicated_queries(q, t_new):
    n_rows = SEG // HEAD_DIM * t_new
    lane_chunk = lax.broadcasted_iota(I32, (n_rows, SEG), 1) // HEAD_DIM
    own = lane_chunk == lax.broadcasted_iota(I32, (n_rows, SEG), 0) // t_new
    return jnp.where(own, jnp.concatenate([q] * (n_rows // t_new), axis=0), 0.0).astype(BF16)


def _pad_rows(x, page):
    return jnp.concatenate([x, jnp.zeros((page - x.shape[0], SEG), F32)], axis=0).astype(BF16)


def _sb_pages(qs, k_refs, v_refs, c, acc_t, uu):
    n_rows, page = qs.shape[0], uu.shape[1]
    z = [_dot(qs, k[...].astype(BF16)) for k in k_refs]
    cinc = _rev_cumsum(jnp.concatenate([_log1m_beta(zu) for zu in z], axis=0), uu)
    pad = jnp.zeros((LANES - n_rows, page), BF16)
    for u, v in enumerate(v_refs):
        cu = cinc[u * n_rows:(u + 1) * n_rows]
        a = jnp.exp(z[u] + c + cu).astype(BF16)
        c = c + cu[:, 0:1]
        acc_t = acc_t + _nt_dot(v[...].astype(BF16), jnp.concatenate([a, pad], axis=0))
    return c, acc_t


def _sb_head_outputs(acc, g, t_new):
    lane = lax.broadcasted_iota(I32, (t_new, SEG), 1)
    o = jnp.zeros((t_new, SEG), F32)
    for h in range(SEG // HEAD_DIM):
        o = o + jnp.where(lane // HEAD_DIM == h, acc[h * t_new:(h + 1) * t_new], 0.0)
    o2 = o * o
    ms = jnp.zeros((t_new, SEG), F32)
    for h in range(SEG // HEAD_DIM):
        sel = lane // HEAD_DIM == h
        ms = jnp.where(sel, jnp.sum(jnp.where(sel, o2, 0.0), axis=-1, keepdims=True), ms)
    return o * lax.rsqrt(ms * (1.0 / HEAD_DIM) + EPS) * g


def _sb_decode_top_kernel(pt_ref, q_ref, nk_ref, nv_ref, g_ref, *rest, pps, page, t_new):
    ck, cv = rest[:pps], rest[pps:2 * pps]
    c_out, acc_out, acct_out, o_ref = rest[2 * pps:]
    n_rows = SEG // HEAD_DIM * t_new
    uu = _cumsum_matrix(page)
    qs = _replicated_queries(q_ref[...], t_new)
    i_q = lax.broadcasted_iota(I32, (n_rows, 1), 0) % t_new
    strict = lax.broadcasted_iota(I32, (n_rows, page), 1) < i_q
    z = _nt_dot(qs, _pad_rows(nk_ref[...], page))
    cinc = _rev_cumsum(jnp.where(strict, _log1m_beta(z), 0.0), uu)
    a = jnp.where(strict, jnp.exp(z + cinc), 0.0).astype(BF16)
    acc_new = _dot(a, _pad_rows(nv_ref[...], page))
    c, acc_t = _sb_pages(qs, ck, cv, cinc[:, 0:1], jnp.zeros((SEG, LANES), F32), uu)
    c_out[...] = jnp.broadcast_to(c, (n_rows, LANES))
    acc_out[...] = acc_new
    acct_out[...] = acc_t
    o_ref[...] = _sb_head_outputs(acc_new + acc_t.T[:n_rows], g_ref[...], t_new)


def _sb_decode_rest_kernel(pt_ref, live_ref, q_ref, c_in, acc_in, acct_in, g_ref, *rest, pps, page, t_new):
    ck, cv = rest[:pps], rest[pps:2 * pps]
    o_ref, qr_ref, c_ref, acct_ref = rest[2 * pps:]
    b, j = pl.program_id(0), pl.program_id(1)
    n_rows = SEG // HEAD_DIM * t_new

    @pl.when(j == 0)
    def _():
        qr_ref[...] = _replicated_queries(q_ref[...], t_new)
        c_ref[...] = c_in[:, 0:1]
        acct_ref[...] = acct_in[...]

    @pl.when(live_ref[b] == 1)
    def _():
        c, acc_t = _sb_pages(qr_ref[...], ck, cv, c_ref[...], acct_ref[...], _cumsum_matrix(page))
        c_ref[...] = c
        acct_ref[...] = acc_t

    @pl.when(j == pl.num_programs(1) - 1)
    def _():
        o_ref[...] = _sb_head_outputs(acc_in[...] + acct_ref[...].T[:n_rows], g_ref[...], t_new)


def _df_decode_kernel(pt_ref, slopes_ref, lam_ref, q_ref, nk_ref, nv_ref, g_ref, *rest,
                      n_pages, pps, page, t_new, lam_init):
    ck, cv = rest[:pps], rest[pps:2 * pps]
    o_ref, qr_ref, m_ref, l_ref, acc_ref = rest[2 * pps:]
    j = pl.program_id(1)
    n_heads = SEG // LANES
    n_rows = SEG // HEAD_DIM * t_new
    past_len = n_pages * page
    rowq = lax.broadcasted_iota(I32, (n_rows, 1), 0)
    i_q = rowq % t_new
    head_of_row = rowq // (2 * t_new)
    slope = jnp.zeros((n_rows, 1), F32)
    for h in range(n_heads):
        slope = jnp.where(head_of_row == h, slopes_ref[h], slope)
    col = lax.broadcasted_iota(I32, (n_rows, page), 1)

    @pl.when(j == 0)
    def _():
        qr_ref[...] = _replicated_queries(q_ref[...], t_new)
        s = _nt_dot(qr_ref[...], _pad_rows(nk_ref[...], page)) - slope * (i_q - col).astype(F32)
        s = jnp.where(col <= i_q, s, NEG_INF)
        m_new = jnp.max(s, axis=-1, keepdims=True)
        p = jnp.exp(s - m_new)
        m_ref[...] = m_new
        l_ref[...] = jnp.sum(p, axis=-1, keepdims=True)
        pv = _dot(p.astype(BF16), _pad_rows(nv_ref[...], page))
        acc = jnp.zeros((n_rows, LANES), F32)
        for h in range(n_heads):
            acc = jnp.where(head_of_row == h, pv[:, h * LANES:(h + 1) * LANES], acc)
        acc_ref[...] = acc

    qd = qr_ref[...]
    s = []
    for u in range(pps):
        lp = n_pages - 1 - (j * pps + u)
        dist = (past_len + i_q - lp * page - col).astype(F32)
        s.append(_dot(qd, ck[u][...].astype(BF16)) - slope * dist)
    smax = s[0]
    for u in range(1, pps):
        smax = jnp.maximum(smax, s[u])
    m_i = m_ref[...]
    m_new = jnp.maximum(m_i, jnp.max(smax, axis=-1, keepdims=True))
    alpha = jnp.exp(m_i - m_new)
    psum = jnp.zeros((n_rows, page), F32)
    acc = alpha * acc_ref[...]
    for u in range(pps):
        p = jnp.exp(s[u] - m_new)
        psum = psum + p
        p_bd = jnp.concatenate([jnp.where(head_of_row == h, p, 0.0) for h in range(n_heads)], axis=1)
        v_hm = jnp.concatenate([cv[u][pl.ds(h, page, stride=n_heads), :] for h in range(n_heads)], axis=0)
        acc = acc + _dot(p_bd.astype(BF16), v_hm.astype(BF16))
    l_ref[...] = alpha * l_ref[...] + jnp.sum(psum, axis=-1, keepdims=True)
    acc_ref[...] = acc
    m_ref[...] = m_new

    @pl.when(j == pl.num_programs(1) - 1)
    def _():
        lam = _lam_value(lam_ref, lam_init)
        acc = acc_ref[...]
        l_i = l_ref[...]
        outs = []
        for h in range(n_heads):
            r0 = 2 * h * t_new
            o0 = acc[r0:r0 + t_new] / l_i[r0:r0 + t_new]
            o1 = acc[r0 + t_new:r0 + 2 * t_new] / l_i[r0 + t_new:r0 + 2 * t_new]
            outs.append(_rms(o0 - lam * o1, g_ref[...]) * (1.0 - lam_init))
        o_ref[...] = jnp.concatenate(outs, axis=1)


def _decode_attention(page_table, slopes, lam_vecs, qsb, qdf, nsbk, nsbv, ndfk, ndfv, g_sb, g_df,
                      c_sbk, c_sbv, c_dfk, c_dfv, lam_init, pages_per_step=16, top_pages=8,
                      rest_pages_per_step=24):
    nb, t_new, _ = qsb.shape
    n_pages = page_table.shape[1]
    page = c_sbk.shape[2]
    pps = pages_per_step
    n_rows = SEG // HEAD_DIM * t_new
    slab = (None, SEG, page)
    o = jax.ShapeDtypeStruct((nb, t_new, SEG), F32)
    params = lambda n: pltpu.CompilerParams(dimension_semantics=("arbitrary",) * n, vmem_limit_bytes=VMEM_LIMIT)

    per_b = pl.BlockSpec((None, t_new, SEG), lambda b, j, pt: (b, 0, 0))
    const2 = lambda shape: pl.BlockSpec(shape, lambda b, j, pt: (0, 0))
    df_pages = [pl.BlockSpec(slab, functools.partial(
        lambda b, j, pt, u: (pt[b, n_pages - 1 - (j * pps + u)], 0, 0), u=u)) for u in range(pps)]
    o_df = pl.pallas_call(
        functools.partial(_df_decode_kernel, n_pages=n_pages, pps=pps, page=page, t_new=t_new,
                          lam_init=lam_init),
        grid_spec=pltpu.PrefetchScalarGridSpec(
            num_scalar_prefetch=1, grid=(nb, n_pages // pps),
            in_specs=[pl.BlockSpec(memory_space=pltpu.SMEM), const2((4, HEAD_DIM)), per_b, per_b, per_b,
                      const2((1, LANES))] + df_pages * 2,
            out_specs=per_b,
            scratch_shapes=[pltpu.VMEM((n_rows, SEG), BF16), pltpu.VMEM((n_rows, 1), F32),
                            pltpu.VMEM((n_rows, 1), F32), pltpu.VMEM((n_rows, LANES), F32)]),
        out_shape=o, compiler_params=params(2), name="df_decode",
    )(page_table, slopes, lam_vecs, qdf, ndfk, ndfv, g_df, *([c_dfk] * pps + [c_dfv] * pps))

    pps = top_pages
    row_b = lambda shape: pl.BlockSpec((None,) + shape, lambda b, pt: (b, 0, 0))
    top_specs = [pl.BlockSpec(slab, functools.partial(
        lambda b, pt, u: (pt[b, n_pages - 1 - u], 0, 0), u=u)) for u in range(pps)]
    state_shapes = [jax.ShapeDtypeStruct((nb, n_rows, LANES), F32), jax.ShapeDtypeStruct((nb, n_rows, SEG), F32),
                    jax.ShapeDtypeStruct((nb, SEG, LANES), F32), o]
    c_top, acc_new, acc_t, o_top = pl.pallas_call(
        functools.partial(_sb_decode_top_kernel, pps=pps, page=page, t_new=t_new),
        grid_spec=pltpu.PrefetchScalarGridSpec(
            num_scalar_prefetch=1, grid=(nb,),
            in_specs=[row_b((t_new, SEG))] * 3 + [pl.BlockSpec((1, SEG), lambda b, pt: (0, 0))] + top_specs * 2,
            out_specs=[row_b((n_rows, LANES)), row_b((n_rows, SEG)), row_b((SEG, LANES)),
                       row_b((t_new, SEG))]),
        out_shape=state_shapes, compiler_params=params(1), name="sb_decode_top",
    )(page_table, qsb, nsbk, nsbv, g_sb, *([c_sbk] * pps + [c_sbv] * pps))

    live = (jnp.max(c_top, axis=(1, 2)) > F32_EXP_ZERO).astype(I32)
    n_rest = n_pages - top_pages
    pps = max(p for p in range(1, rest_pages_per_step + 1) if n_rest % p == 0)

    def rest_index(b, j, pt, live, u):
        lp = n_rest - 1 - (j * pps + u)
        return (jnp.where(live[b] == 1, pt[b, lp], pt[0, 0]), 0, 0)

    rest_specs = [pl.BlockSpec(slab, functools.partial(rest_index, u=u)) for u in range(pps)]
    row_bj = lambda shape: pl.BlockSpec((None,) + shape, lambda b, j, pt, live: (b, 0, 0))

    def walk_rest():
        return pl.pallas_call(
            functools.partial(_sb_decode_rest_kernel, pps=pps, page=page, t_new=t_new),
            grid_spec=pltpu.PrefetchScalarGridSpec(
                num_scalar_prefetch=2, grid=(nb, n_rest // pps),
                in_specs=[row_bj((t_new, SEG)), row_bj((n_rows, LANES)), row_bj((n_rows, SEG)),
                          row_bj((SEG, LANES)), pl.BlockSpec((1, SEG), lambda b, j, pt, live: (0, 0))]
                         + rest_specs * 2,
                out_specs=row_bj((t_new, SEG)),
                scratch_shapes=[pltpu.VMEM((n_rows, SEG), BF16), pltpu.VMEM((n_rows, 1), F32),
                                pltpu.VMEM((SEG, LANES), F32)]),
            out_shape=o, compiler_params=params(2), name="sb_decode_rest",
        )(page_table, live, qsb, c_top, acc_new, acc_t, g_sb, *([c_sbk] * pps + [c_sbv] * pps))

    o_sb = lax.cond(jnp.any(live == 1), walk_rest, lambda: o_top)
    return o_sb, o_df


def _gelu_tanh(x):
    return 0.5 * x * (1.0 + jnp.tanh(math.sqrt(2.0 / math.pi) * (x + 0.044715 * x * x * x)))


def _post_kernel(x_ref, msb_ref, mdf_ref, woa_ref, wob_ref, gffn_ref, wup_ref, cw_ref, cb_ref, wdn_ref,
                 gfin_ref, *rest, d_ff, fc, seq_rows):
    if seq_rows is None:
        y_ref, conv_ref, carry_ref = rest
    else:
        p1_ref, p2_ref, y_ref, conv_ref = rest
    tm = x_ref.shape[0]
    h1 = x_ref[...] + _dot(msb_ref[...].astype(BF16), woa_ref[...]) + _dot(mdf_ref[...].astype(BF16), wob_ref[...])
    h2 = _rms(h1, gffn_ref[...]).astype(BF16)
    rowi = lax.broadcasted_iota(I32, (tm, fc), 0)

    if seq_rows is None:
        @pl.when(pl.program_id(1) == 0)
        def _():
            carry_ref[...] = jnp.zeros_like(carry_ref)

    def conv(cols):
        u = _dot(h2, wup_ref[:, cols])
        r1 = pltpu.roll(u, 1, axis=0)
        r2 = pltpu.roll(u, 2, axis=0)
        if seq_rows is None:
            prev = carry_ref[:, cols]
            r1 = jnp.where(rowi == 0, prev[7:8], r1)
            r2 = jnp.where(rowi == 0, prev[6:7], jnp.where(rowi == 1, prev[7:8], r2))
            carry_ref[:, cols] = u[tm - 8:]
            conv_ref[:, cols] = u[tm - 8:]
        else:
            pos = rowi % seq_rows
            r1 = jnp.where(pos == 0, p1_ref[:, cols], r1)
            r2 = jnp.where(pos < 2, p2_ref[:, cols], r2)
            conv_ref[:, cols] = u
        cw = cw_ref[:, cols]
        return cb_ref[:, cols] + cw[2:3] * u + cw[1:2] * r1 + cw[0:1] * r2

    acc = jnp.zeros((tm, x_ref.shape[1]), F32)
    for c in range(d_ff // fc):
        gate = conv(slice(c * fc, (c + 1) * fc))
        val = conv(slice(d_ff + c * fc, d_ff + (c + 1) * fc))
        act = (_gelu_tanh(gate) * val).astype(BF16)
        acc = acc + _dot(act, wdn_ref[c * fc:(c + 1) * fc, :])
    y_ref[...] = _rms(h1 + acc, gfin_ref[...])


def _post(x, msb, mdf, woa, wob, gffn, wup, cw, cb, wdn, gfin, prev=None, *, tm, fc=1408):
    d = x.shape[-1]
    d_ff = wdn.shape[0]
    weights = (woa, wob, gffn, wup, cw, cb, wdn, gfin)
    if prev is None:
        b, t, _ = x.shape
        grid = (b, t // tm)
        act = lambda w: pl.BlockSpec((None, tm, w), lambda bi, ti: (bi, ti, 0))
        const = lambda a: pl.BlockSpec(a.shape, lambda bi, ti: (0,) * a.ndim, pipeline_mode=pl.Buffered(1))
        in_specs = [act(d), act(SEG), act(SEG)] + [const(a) for a in weights]
        out_specs = [act(d), pl.BlockSpec((None, 8, 2 * d_ff), lambda bi, ti: (bi, 0, 0))]
        out_shape = [jax.ShapeDtypeStruct((b, t, d), F32), jax.ShapeDtypeStruct((b, 8, 2 * d_ff), F32)]
        scratch = [pltpu.VMEM((8, 2 * d_ff), F32)]
        args = (x, msb, mdf) + weights
        kern = functools.partial(_post_kernel, d_ff=d_ff, fc=fc, seq_rows=None)
        name = "post_prompt"
    else:
        p1, p2, seq_rows = prev
        m = x.shape[0]
        grid = (m // tm, 1)
        act = lambda w: pl.BlockSpec((tm, w), lambda mi, ti: (mi, 0))
        const = lambda a: pl.BlockSpec(a.shape, lambda mi, ti: (0,) * a.ndim)
        in_specs = ([act(d), act(SEG), act(SEG)] + [const(a) for a in weights]
                    + [act(2 * d_ff), act(2 * d_ff)])
        out_specs = [act(d), act(2 * d_ff)]
        out_shape = [jax.ShapeDtypeStruct((m, d), F32), jax.ShapeDtypeStruct((m, 2 * d_ff), F32)]
        scratch = []
        args = (x, msb, mdf) + weights + (p1, p2)
        kern = functools.partial(_post_kernel, d_ff=d_ff, fc=fc, seq_rows=seq_rows)
        name = "post_sample"
    return pl.pallas_call(
        kern, grid=grid, in_specs=in_specs, out_specs=out_specs, out_shape=out_shape,
        scratch_shapes=scratch,
        compiler_params=pltpu.CompilerParams(dimension_semantics=("arbitrary", "arbitrary"),
                                             vmem_limit_bytes=56 * 1024 * 1024),
        name=name,
    )(*args)


def kernel(x_prompt, x_sample, cache_sb_k, cache_sb_v, cache_df_k, cache_df_v, state_conv, page_table,
           norm_mix, w_in, sb_head_norm, df_head_norm, lam_q1, lam_k1, lam_q2, lam_k2, w_o,
           norm_ffn, w_up, conv_w, conv_b, w_down, norm_final):
    depth = w_in.shape[0]
    assert depth == 1, "single-layer step"
    b, t, d = x_prompt.shape
    nb, t_new, _ = x_sample.shape
    n_pool, page = cache_sb_k.shape[1], cache_sb_k.shape[2]
    d_ff = w_down.shape[1]
    sb_heads = cache_sb_k.shape[3]
    df_heads = cache_df_k.shape[3]
    lam_init = 0.8 - 0.6 * math.exp(-0.3 * 0)
    slopes = jnp.asarray(2.0 ** (-8.0 * np.arange(1, df_heads + 1) / df_heads), dtype=F32)

    g_mix = norm_mix[0].reshape(1, d)
    w_in_b = w_in[0].astype(BF16)
    g_sb = jnp.tile(sb_head_norm[0], LANES // HEAD_DIM).reshape(1, LANES)
    g_sb_full = jnp.tile(sb_head_norm[0], SEG // HEAD_DIM).reshape(1, SEG)
    g_df = df_head_norm[0].reshape(1, LANES)
    lam_vecs = jnp.stack([lam_q1[0], lam_k1[0], lam_q2[0], lam_k2[0]]).astype(F32)
    wo_b = w_o[0].astype(BF16)
    post_w = (wo_b[:SEG], wo_b[SEG:], norm_ffn[0].reshape(1, d), w_up[0].astype(BF16), conv_w[0],
              conv_b[0].reshape(1, 2 * d_ff), w_down[0].astype(BF16), norm_final.reshape(1, d))

    (p_sbk, p_sbv, p_dfk, p_dfv, qsb, ksb, vsb, qdf, kdf, vdf) = _proj(
        x_prompt, g_mix, w_in_b, BF16, stored_layout=True, tm=PROMPT_TM)
    p_sbk = jnp.moveaxis(p_sbk.reshape(b, sb_heads, HEAD_DIM, t), 3, 1)
    p_sbv = jnp.moveaxis(p_sbv.reshape(b, sb_heads, HEAD_DIM, t), 3, 1)
    p_dfk = jnp.moveaxis(p_dfk.reshape(b, df_heads, 2, HEAD_DIM, t), 4, 1)
    o_sb, o_df = _prompt_attention(qsb, ksb, vsb, qdf, kdf, vdf, g_sb, g_df, slopes, lam_vecs, lam_init)
    y_prompt, conv_tail = _post(x_prompt, o_sb, o_df, *post_w, tm=PROMPT_TM)
    p_conv = conv_tail[:, 8 - (CONV_W - 1):]

    (s_sbk, s_sbv, s_dfk, s_dfv, sqsb, _, _, sqdf, _, _) = _proj(
        x_sample.reshape(1, nb * t_new, d), g_mix, w_in_b, F32, stored_layout=False, tm=nb * t_new)
    s3 = lambda a: a.reshape(nb, t_new, SEG)
    pos_minor = lambda c: jnp.moveaxis(c.reshape(n_pool, page, SEG), 1, 2)
    so_sb, so_df = _decode_attention(
        page_table, slopes, lam_vecs, s3(sqsb), s3(sqdf), s3(s_sbk), s3(s_sbv), s3(s_dfk), s3(s_dfv),
        g_sb_full, g_df, pos_minor(cache_sb_k), pos_minor(cache_sb_v), pos_minor(cache_df_k),
        cache_df_v.reshape(n_pool, page * df_heads, 2 * HEAD_DIM), lam_init)
    st = state_conv[0]
    zeros = jnp.zeros((nb, t_new - 1, 2 * d_ff), F32)
    prev1 = jnp.concatenate([st[:, 1:2], zeros], axis=1).reshape(nb * t_new, 2 * d_ff)
    prev2 = jnp.concatenate([st, zeros[:, 1:]], axis=1).reshape(nb * t_new, 2 * d_ff)
    y_sample, u_s = _post(x_sample.reshape(nb * t_new, d), so_sb.reshape(nb * t_new, SEG),
                          so_df.reshape(nb * t_new, SEG), *post_w, prev=(prev1, prev2, t_new),
                          tm=nb * t_new)
    s_conv = u_s.reshape(nb, t_new, 2 * d_ff)[:, t_new - (CONV_W - 1):]

    return (y_prompt, y_sample.reshape(nb, t_new, d),
            p_sbk[None], p_sbv[None], p_dfk[None], p_dfv.reshape(1, b, t, df_heads, 2 * HEAD_DIM),
            p_conv[None],
            s_sbk.reshape(1, nb, t_new, sb_heads, HEAD_DIM), s_sbv.reshape(1, nb, t_new, sb_heads, HEAD_DIM),
            s_dfk.reshape(1, nb, t_new, df_heads, 2, HEAD_DIM), s_dfv.reshape(1, nb, t_new, df_heads, 2 * HEAD_DIM),
            s_conv[None])
```

```python
import functools
import math

import numpy as np
import jax
import jax.numpy as jnp
from jax import lax
from jax.experimental import pallas as pl
from jax.experimental.pallas import tpu as pltpu

F32 = jnp.float32
BF16 = jnp.bfloat16
I32 = jnp.int32

EPS = 1e-6
HEAD_DIM = 64
LANES = 128
SEG = 512
N_SEG = 6
CONV_W = 3
NEG_INF = float("-inf")
VMEM_LIMIT = 48 * 1024 * 1024
PROMPT_TM = 512
F32_EXP_ZERO = -104.0


def _rms(x, g):
    ms = jnp.mean(x * x, axis=-1, keepdims=True)
    return x * lax.rsqrt(ms + EPS) * g


def _nt_dot(a, b):
    return lax.dot_general(a, b, (((1,), (1,)), ((), ())), preferred_element_type=F32)


def _dot(a, b):
    return jnp.dot(a, b, preferred_element_type=F32)


def _cumsum_matrix(tk):
    j = lax.broadcasted_iota(I32, (2 * tk, tk), 0)
    s = lax.broadcasted_iota(I32, (2 * tk, tk), 1)
    j = jnp.where(j >= tk, j - tk, j)
    return jnp.where(j >= s, 1.0, 0.0).astype(BF16)


def _rev_cumsum(l, uu):
    hi = l.astype(BF16)
    lo = (l - hi.astype(F32)).astype(BF16)
    return _dot(jnp.concatenate([hi, lo], axis=1), uu)


def _log1m_beta(z):
    return -(jnp.maximum(z, 0.0) + jnp.log(1.0 + jnp.exp(-jnp.abs(z))))


def _proj_kernel(x_ref, g_ref, w_ref, sbk_ref, sbv_ref, dfk_ref, dfv_ref,
                 qsb_ref, ksb_ref, vsb_ref, qdf_ref, kdf_ref, vdf_ref, *, stored_layout):
    h = _rms(x_ref[...], g_ref[...]).astype(BF16)
    tm = h.shape[0]

    def seg(j):
        return _dot(h, w_ref[:, j * SEG:(j + 1) * SEG])

    def put(ref, val):
        ref[...] = val.T if stored_layout else val

    scale = HEAD_DIM ** -0.5
    qsb_ref[...] = (seg(0) * scale).astype(qsb_ref.dtype)
    k = seg(1)
    put(sbk_ref, k)
    ksb_ref[...] = k.astype(BF16)
    v = seg(2)
    put(sbv_ref, v)
    vsb_ref[...] = v.astype(BF16)
    qdf_ref[...] = (seg(3) * scale).astype(qdf_ref.dtype)
    k = seg(4)
    put(dfk_ref, k)
    kdf_ref[...] = k.astype(BF16)
    v = seg(5)
    if stored_layout:
        n_heads = SEG // LANES
        for hd in range(n_heads):
            dfv_ref[pl.ds(hd, tm, stride=n_heads), :] = v[:, hd * LANES:(hd + 1) * LANES]
    else:
        dfv_ref[...] = v
    vdf_ref[...] = v.astype(BF16)


def _proj(x, g, w, q_dtype, stored_layout, tm):
    b, t, d = x.shape
    grid = (b, t // tm)
    rows = pl.BlockSpec((None, tm, SEG), lambda bi, ti: (bi, ti, 0))
    const = lambda bi, ti: (0, 0)
    f32o = jax.ShapeDtypeStruct((b, t, SEG), F32)
    b16o = jax.ShapeDtypeStruct((b, t, SEG), BF16)
    qo = jax.ShapeDtypeStruct((b, t, SEG), q_dtype)
    if stored_layout:
        n_heads = SEG // LANES
        tspec = pl.BlockSpec((None, SEG, tm), lambda bi, ti: (bi, 0, ti))
        to = jax.ShapeDtypeStruct((b, SEG, t), F32)
        kv_specs = [tspec, tspec, tspec, pl.BlockSpec((None, tm * n_heads, LANES), lambda bi, ti: (bi, ti, 0))]
        kv_shapes = [to, to, to, jax.ShapeDtypeStruct((b, t * n_heads, LANES), F32)]
    else:
        kv_specs = [rows] * 4
        kv_shapes = [f32o] * 4
    return pl.pallas_call(
        functools.partial(_proj_kernel, stored_layout=stored_layout),
        grid=grid,
        in_specs=[pl.BlockSpec((None, tm, d), lambda bi, ti: (bi, ti, 0)), pl.BlockSpec((1, d), const),
                  pl.BlockSpec((d, N_SEG * SEG), const, pipeline_mode=pl.Buffered(1))],
        out_specs=kv_specs + [rows] * 6,
        out_shape=kv_shapes + [qo, b16o, b16o, qo, b16o, b16o],
        compiler_params=pltpu.CompilerParams(dimension_semantics=("arbitrary", "arbitrary"),
                                             vmem_limit_bytes=VMEM_LIMIT),
        name="proj",
    )(x, g, w)


def _split_lane_halves_rows(x):
    lo = lax.broadcasted_iota(I32, x.shape, 1) < HEAD_DIM
    zero = jnp.zeros_like(x)
    return jnp.concatenate([jnp.where(lo, x, zero), jnp.where(lo, zero, x)], axis=0)


def _sb_prompt_kernel(q_ref, k_ref, v_ref, g_ref, o_ref, q2_ref, c_ref, acc_ref, *, tq, tk):
    qi = pl.program_id(1)
    n_blk = SEG // LANES
    for p in range(n_blk):
        q2_ref[p] = _split_lane_halves_rows(q_ref[:, p * LANES:(p + 1) * LANES])
    c_ref[...] = jnp.zeros_like(c_ref)
    acc_ref[...] = jnp.zeros_like(acc_ref)
    uu = _cumsum_matrix(tk)
    row = lax.broadcasted_iota(I32, (2 * tq, tk), 0)
    col = lax.broadcasted_iota(I32, (2 * tq, tk), 1)
    qpos = qi * tq + jnp.where(row >= tq, row - tq, row)

    def block(kj, masked):
        start = pl.multiple_of(kj * tk, tk)
        if masked:
            strict = start + col < qpos
        for p in range(n_blk):
            lanes = slice(p * LANES, (p + 1) * LANES)
            z = _nt_dot(q2_ref[p], k_ref[pl.ds(start, tk), lanes])
            l = _log1m_beta(z)
            if masked:
                l = jnp.where(strict, l, 0.0)
            cinc = _rev_cumsum(l, uu)
            c = c_ref[p]
            a = jnp.exp(z + c + cinc)
            if masked:
                a = jnp.where(strict, a, 0.0)
            c_ref[p] = c + cinc[:, 0:1]
            ab = a.astype(BF16)
            a_cat = jnp.concatenate([ab[:tq], ab[tq:]], axis=1)
            acc_ref[p] += _dot(a_cat, _split_lane_halves_rows(v_ref[pl.ds(start, tk), lanes]))

    n_diag = tq // tk
    top = (qi + 1) * n_diag - 1
    for d in range(n_diag):
        block(top - d, True)

    def c_max():
        c = c_ref[0]
        for p in range(1, n_blk):
            c = jnp.maximum(c, c_ref[p])
        return jnp.max(c)

    def body(carry):
        kj, _ = carry
        block(kj, False)
        return kj - 1, c_max()

    lax.while_loop(lambda carry: (carry[0] >= 0) & (carry[1] > F32_EXP_ZERO), body,
                   (top - n_diag, c_max()))

    lo = lax.broadcasted_iota(I32, (tq, LANES), 1) < HEAD_DIM
    for p in range(n_blk):
        o = acc_ref[p]
        o2 = o * o
        s_lo = jnp.sum(jnp.where(lo, o2, 0.0), axis=-1, keepdims=True)
        s_hi = jnp.sum(jnp.where(lo, 0.0, o2), axis=-1, keepdims=True)
        ms = jnp.where(lo, s_lo, s_hi) * (1.0 / HEAD_DIM)
        o_ref[:, p * LANES:(p + 1) * LANES] = (o * lax.rsqrt(ms + EPS) * g_ref[...]).astype(o_ref.dtype)


def _lam_value(lam_ref, lam_init):
    lv = lam_ref[...]
    s1 = jnp.sum(lv[0:1] * lv[1:2], axis=-1, keepdims=True)
    s2 = jnp.sum(lv[2:3] * lv[3:4], axis=-1, keepdims=True)
    return jnp.exp(s1) - jnp.exp(s2) + lam_init


def _df_prompt_kernel(slopes_ref, lam_ref, q_ref, k_ref, v_ref, g_ref, o_ref, q2_ref, m_ref, acc_ref,
                      *, tq, tk, lam_init):
    qi = pl.program_id(1)
    n_heads = SEG // LANES
    for h in range(n_heads):
        q2_ref[h] = _split_lane_halves_rows(q_ref[:, h * LANES:(h + 1) * LANES])
    m_ref[...] = jnp.full_like(m_ref, NEG_INF)
    acc_ref[...] = jnp.zeros_like(acc_ref)
    row = lax.broadcasted_iota(I32, (2 * tq, tk), 0)
    col = lax.broadcasted_iota(I32, (2 * tq, tk), 1)
    rel = jnp.where(row >= tq, row - tq, row) - col
    ones = jnp.ones((tk, LANES), BF16)

    def block(kj, masked):
        start = pl.multiple_of(kj * tk, tk)
        dist_i = rel + (qi * tq - start)
        dist = dist_i.astype(F32)
        for h in range(n_heads):
            lanes = slice(h * LANES, (h + 1) * LANES)
            s = _nt_dot(q2_ref[h], k_ref[pl.ds(start, tk), lanes]) - slopes_ref[h] * dist
            if masked:
                s = jnp.where(dist_i >= 0, s, NEG_INF)
            m_i = m_ref[h]
            m_new = jnp.maximum(m_i, jnp.max(s, axis=-1, keepdims=True))
            alpha = jnp.exp(m_i - m_new)
            p = jnp.exp(s - m_new).astype(BF16)
            v_aug = jnp.concatenate([v_ref[pl.ds(start, tk), lanes], ones], axis=1)
            acc_ref[h] = alpha * acc_ref[h] + _dot(p, v_aug)
            m_ref[h] = m_new

    top = ((qi + 1) * tq - 1) // tk
    block(top, True)
    lax.fori_loop(0, top, lambda i, _: block(top - 1 - i, False), None)

    lam = _lam_value(lam_ref, lam_init)
    for h in range(n_heads):
        acc = acc_ref[h]
        r = acc[:, :LANES] / acc[:, LANES:]
        o = r[:tq] - lam * r[tq:]
        o_ref[:, h * LANES:(h + 1) * LANES] = (_rms(o, g_ref[...]) * (1.0 - lam_init)).astype(o_ref.dtype)


def _prompt_attention(qsb, ksb, vsb, qdf, kdf, vdf, g_sb, g_df, slopes, lam_vecs, lam_init,
                      tq_sb=256, tk_sb=256, tq_df=256, tk_df=1024):
    b, t, _ = qsb.shape
    n_blk = SEG // LANES
    qspec = lambda tq: pl.BlockSpec((None, tq, SEG), lambda bi, qi: (bi, qi, 0))
    kvspec = pl.BlockSpec((None, t, SEG), lambda bi, qi: (bi, 0, 0))
    gspec = pl.BlockSpec((1, LANES), lambda bi, qi: (0, 0))
    out = jax.ShapeDtypeStruct((b, t, SEG), BF16)
    params = pltpu.CompilerParams(dimension_semantics=("arbitrary",) * 2, vmem_limit_bytes=VMEM_LIMIT)
    q2_scratch = lambda tq: pltpu.VMEM((n_blk, 2 * tq, LANES), BF16)
    o_sb = pl.pallas_call(
        functools.partial(_sb_prompt_kernel, tq=tq_sb, tk=tk_sb),
        grid=(b, t // tq_sb), in_specs=[qspec(tq_sb), kvspec, kvspec, gspec], out_specs=qspec(tq_sb),
        out_shape=out,
        scratch_shapes=[q2_scratch(tq_sb), pltpu.VMEM((n_blk, 2 * tq_sb, 1), F32),
                        pltpu.VMEM((n_blk, tq_sb, LANES), F32)],
        compiler_params=params, name="sb_prompt",
    )(qsb, ksb, vsb, g_sb)
    o_df = pl.pallas_call(
        functools.partial(_df_prompt_kernel, tq=tq_df, tk=tk_df, lam_init=lam_init),
        grid=(b, t // tq_df),
        in_specs=[pl.BlockSpec(memory_space=pltpu.SMEM),
                  pl.BlockSpec((4, HEAD_DIM), lambda bi, qi: (0, 0)),
                  qspec(tq_df), kvspec, kvspec, gspec],
        out_specs=qspec(tq_df), out_shape=out,
        scratch_shapes=[q2_scratch(tq_df), pltpu.VMEM((n_blk, 2 * tq_df, 1), F32),
                        pltpu.VMEM((n_blk, 2 * tq_df, 2 * LANES), F32)],
        compiler_params=params, name="df_prompt",
    )(slopes, lam_vecs, qdf, kdf, vdf, g_df)
    return o_sb, o_df


def _replicated_queries(q, t_new):
    n_rows = SEG // HEAD_DIM * t_new
    lane_chunk = lax.broadcasted_iota(I32, (n_rows, SEG), 1) // HEAD_DIM
    own = lane_chunk == lax.broadcasted_iota(I32, (n_rows, SEG), 0) // t_new
    return jnp.where(own, jnp.concatenate([q] * (n_rows // t_new), axis=0), 0.0).astype(BF16)


def _pad_rows(x, page):
    return jnp.concatenate([x, jnp.zeros((page - x.shape[0], SEG), F32)], axis=0).astype(BF16)


def _sb_pages(qs, k_refs, v_refs, c, acc_t, uu):
    n_rows, page = qs.shape[0], uu.shape[1]
    z = [_dot(qs, k[...].astype(BF16)) for k in k_refs]
    cinc = _rev_cumsum(jnp.concatenate([_log1m_beta(zu) for zu in z], axis=0), uu)
    pad = jnp.zeros((LANES - n_rows, page), BF16)
    for u, v in enumerate(v_refs):
        cu = cinc[u * n_rows:(u + 1) * n_rows]
        a = jnp.exp(z[u] + c + cu).astype(BF16)
        c = c + cu[:, 0:1]
        acc_t = acc_t + _nt_dot(v[...].astype(BF16), jnp.concatenate([a, pad], axis=0))
    return c, acc_t


def _sb_head_outputs(acc, g, t_new):
    lane = lax.broadcasted_iota(I32, (t_new, SEG), 1)
    o = jnp.zeros((t_new, SEG), F32)
    for h in range(SEG // HEAD_DIM):
        o = o + jnp.where(lane // HEAD_DIM == h, acc[h * t_new:(h + 1) * t_new], 0.0)
    o2 = o * o
    ms = jnp.zeros((t_new, SEG), F32)
    for h in range(SEG // HEAD_DIM):
        sel = lane // HEAD_DIM == h
        ms = jnp.where(sel, jnp.sum(jnp.where(sel, o2, 0.0), axis=-1, keepdims=True), ms)
    return o * lax.rsqrt(ms * (1.0 / HEAD_DIM) + EPS) * g


def _sb_decode_top_kernel(pt_ref, q_ref, nk_ref, nv_ref, g_ref, *rest, pps, page, t_new):
    ck, cv = rest[:pps], rest[pps:2 * pps]
    c_out, acc_out, acct_out, o_ref = rest[2 * pps:]
    n_rows = SEG // HEAD_DIM * t_new
    uu = _cumsum_matrix(page)
    qs = _replicated_queries(q_ref[...], t_new)
    i_q = lax.broadcasted_iota(I32, (n_rows, 1), 0) % t_new
    strict = lax.broadcasted_iota(I32, (n_rows, page), 1) < i_q
    z = _nt_dot(qs, _pad_rows(nk_ref[...], page))
    cinc = _rev_cumsum(jnp.where(strict, _log1m_beta(z), 0.0), uu)
    a = jnp.where(strict, jnp.exp(z + cinc), 0.0).astype(BF16)
    acc_new = _dot(a, _pad_rows(nv_ref[...], page))
    c, acc_t = _sb_pages(qs, ck, cv, cinc[:, 0:1], jnp.zeros((SEG, LANES), F32), uu)
    c_out[...] = jnp.broadcast_to(c, (n_rows, LANES))
    acc_out[...] = acc_new
    acct_out[...] = acc_t
    o_ref[...] = _sb_head_outputs(acc_new + acc_t.T[:n_rows], g_ref[...], t_new)


def _sb_decode_rest_kernel(pt_ref, live_ref, q_ref, c_in, acc_in, acct_in, g_ref, *rest, pps, page, t_new):
    ck, cv = rest[:pps], rest[pps:2 * pps]
    o_ref, qr_ref, c_ref, acct_ref = rest[2 * pps:]
    b, j = pl.program_id(0), pl.program_id(1)
    n_rows = SEG // HEAD_DIM * t_new

    @pl.when(j == 0)
    def _():
        qr_ref[...] = _replicated_queries(q_ref[...], t_new)
        c_ref[...] = c_in[:, 0:1]
        acct_ref[...] = acct_in[...]

    @pl.when(live_ref[b] == 1)
    def _():
        c, acc_t = _sb_pages(qr_ref[...], ck, cv, c_ref[...], acct_ref[...], _cumsum_matrix(page))
        c_ref[...] = c
        acct_ref[...] = acc_t

    @pl.when(j == pl.num_programs(1) - 1)
    def _():
        o_ref[...] = _sb_head_outputs(acc_in[...] + acct_ref[...].T[:n_rows], g_ref[...], t_new)


def _df_decode_kernel(pt_ref, slopes_ref, lam_ref, q_ref, nk_ref, nv_ref, g_ref, *rest,
                      n_pages, pps, page, t_new, lam_init):
    ck, cv = rest[:pps], rest[pps:2 * pps]
    o_ref, qr_ref, m_ref, l_ref, acc_ref = rest[2 * pps:]
    j = pl.program_id(1)
    n_heads = SEG // LANES
    n_rows = SEG // HEAD_DIM * t_new
    past_len = n_pages * page
    rowq = lax.broadcasted_iota(I32, (n_rows, 1), 0)
    i_q = rowq % t_new
    head_of_row = rowq // (2 * t_new)
    slope = jnp.zeros((n_rows, 1), F32)
    for h in range(n_heads):
        slope = jnp.where(head_of_row == h, slopes_ref[h], slope)
    col = lax.broadcasted_iota(I32, (n_rows, page), 1)

    @pl.when(j == 0)
    def _():
        qr_ref[...] = _replicated_queries(q_ref[...], t_new)
        s = _nt_dot(qr_ref[...], _pad_rows(nk_ref[...], page)) - slope * (i_q - col).astype(F32)
        s = jnp.where(col <= i_q, s, NEG_INF)
        m_new = jnp.max(s, axis=-1, keepdims=True)
        p = jnp.exp(s - m_new)
        m_ref[...] = m_new
        l_ref[...] = jnp.sum(p, axis=-1, keepdims=True)
        pv = _dot(p.astype(BF16), _pad_rows(nv_ref[...], page))
        acc = jnp.zeros((n_rows, LANES), F32)
        for h in range(n_heads):
            acc = jnp.where(head_of_row == h, pv[:, h * LANES:(h + 1) * LANES], acc)
        acc_ref[...] = acc

    qd = qr_ref[...]
    s = []
    for u in range(pps):
        lp = n_pages - 1 - (j * pps + u)
        dist = (past_len + i_q - lp * page - col).astype(F32)
        s.append(_dot(qd, ck[u][...].astype(BF16)) - slope * dist)
    smax = s[0]
    for u in range(1, pps):
        smax = jnp.maximum(smax, s[u])
    m_i = m_ref[...]
    m_new = jnp.maximum(m_i, jnp.max(smax, axis=-1, keepdims=True))
    alpha = jnp.exp(m_i - m_new)
    psum = jnp.zeros((n_rows, page), F32)
    acc = alpha * acc_ref[...]
    for u in range(pps):
        p = jnp.exp(s[u] - m_new)
        psum = psum + p
        p_bd = jnp.concatenate([jnp.where(head_of_row == h, p, 0.0) for h in range(n_heads)], axis=1)
        v_hm = jnp.concatenate([cv[u][pl.ds(h, page, stride=n_heads), :] for h in range(n_heads)], axis=0)
        acc = acc + _dot(p_bd.astype(BF16), v_hm.astype(BF16))
    l_ref[...] = alpha * l_ref[...] + jnp.sum(psum, axis=-1, keepdims=True)
    acc_ref[...] = acc
    m_ref[...] = m_new

    @pl.when(j == pl.num_programs(1) - 1)
    def _():
        lam = _lam_value(lam_ref, lam_init)
        acc = acc_ref[...]
        l_i = l_ref[...]
        outs = []
        for h in range(n_heads):
            r0 = 2 * h * t_new
            o0 = acc[r0:r0 + t_new] / l_i[r0:r0 + t_new]
            o1 = acc[r0 + t_new:r0 + 2 * t_new] / l_i[r0 + t_new:r0 + 2 * t_new]
            outs.append(_rms(o0 - lam * o1, g_ref[...]) * (1.0 - lam_init))
        o_ref[...] = jnp.concatenate(outs, axis=1)


def _decode_attention(page_table, slopes, lam_vecs, qsb, qdf, nsbk, nsbv, ndfk, ndfv, g_sb, g_df,
                      c_sbk, c_sbv, c_dfk, c_dfv, lam_init, pages_per_step=32, top_pages=4,
                      rest_pages_per_step=31):
    nb, t_new, _ = qsb.shape
    n_pages = page_table.shape[1]
    page = c_sbk.shape[2]
    pps = pages_per_step
    n_rows = SEG // HEAD_DIM * t_new
    slab = (None, SEG, page)
    o = jax.ShapeDtypeStruct((nb, t_new, SEG), F32)
    params = lambda n: pltpu.CompilerParams(dimension_semantics=("arbitrary",) * n, vmem_limit_bytes=VMEM_LIMIT)

    per_b = pl.BlockSpec((None, t_new, SEG), lambda b, j, pt: (b, 0, 0))
    const2 = lambda shape: pl.BlockSpec(shape, lambda b, j, pt: (0, 0))
    df_pages = [pl.BlockSpec(slab, functools.partial(
        lambda b, j, pt, u: (pt[b, n_pages - 1 - (j * pps + u)], 0, 0), u=u)) for u in range(pps)]
    o_df = pl.pallas_call(
        functools.partial(_df_decode_kernel, n_pages=n_pages, pps=pps, page=page, t_new=t_new,
                          lam_init=lam_init),
        grid_spec=pltpu.PrefetchScalarGridSpec(
            num_scalar_prefetch=1, grid=(nb, n_pages // pps),
            in_specs=[pl.BlockSpec(memory_space=pltpu.SMEM), const2((4, HEAD_DIM)), per_b, per_b, per_b,
                      const2((1, LANES))] + df_pages * 2,
            out_specs=per_b,
            scratch_shapes=[pltpu.VMEM((n_rows, SEG), BF16), pltpu.VMEM((n_rows, 1), F32),
                            pltpu.VMEM((n_rows, 1), F32), pltpu.VMEM((n_rows, LANES), F32)]),
        out_shape=o, compiler_params=params(2), name="df_decode",
    )(page_table, slopes, lam_vecs, qdf, ndfk, ndfv, g_df, *([c_dfk] * pps + [c_dfv] * pps))

    pps = top_pages
    row_b = lambda shape: pl.BlockSpec((None,) + shape, lambda b, pt: (b, 0, 0))
    top_specs = [pl.BlockSpec(slab, functools.partial(
        lambda b, pt, u: (pt[b, n_pages - 1 - u], 0, 0), u=u)) for u in range(pps)]
    state_shapes = [jax.ShapeDtypeStruct((nb, n_rows, LANES), F32), jax.ShapeDtypeStruct((nb, n_rows, SEG), F32),
                    jax.ShapeDtypeStruct((nb, SEG, LANES), F32), o]
    c_top, acc_new, acc_t, o_top = pl.pallas_call(
        functools.partial(_sb_decode_top_kernel, pps=pps, page=page, t_new=t_new),
        grid_spec=pltpu.PrefetchScalarGridSpec(
            num_scalar_prefetch=1, grid=(nb,),
            in_specs=[row_b((t_new, SEG))] * 3 + [pl.BlockSpec((1, SEG), lambda b, pt: (0, 0))] + top_specs * 2,
            out_specs=[row_b((n_rows, LANES)), row_b((n_rows, SEG)), row_b((SEG, LANES)),
                       row_b((t_new, SEG))]),
        out_shape=state_shapes, compiler_params=params(1), name="sb_decode_top",
    )(page_table, qsb, nsbk, nsbv, g_sb, *([c_sbk] * pps + [c_sbv] * pps))

    live = (jnp.max(c_top, axis=(1, 2)) > F32_EXP_ZERO).astype(I32)
    n_rest = n_pages - top_pages
    pps = max(p for p in range(1, rest_pages_per_step + 1) if n_rest % p == 0)

    def rest_index(b, j, pt, live, u):
        lp = n_rest - 1 - (j * pps + u)
        return (jnp.where(live[b] == 1, pt[b, lp], pt[0, 0]), 0, 0)

    rest_specs = [pl.BlockSpec(slab, functools.partial(rest_index, u=u)) for u in range(pps)]
    row_bj = lambda shape: pl.BlockSpec((None,) + shape, lambda b, j, pt, live: (b, 0, 0))

    def walk_rest():
        return pl.pallas_call(
            functools.partial(_sb_decode_rest_kernel, pps=pps, page=page, t_new=t_new),
            grid_spec=pltpu.PrefetchScalarGridSpec(
                num_scalar_prefetch=2, grid=(nb, n_rest // pps),
                in_specs=[row_bj((t_new, SEG)), row_bj((n_rows, LANES)), row_bj((n_rows, SEG)),
                          row_bj((SEG, LANES)), pl.BlockSpec((1, SEG), lambda b, j, pt, live: (0, 0))]
                         + rest_specs * 2,
                out_specs=row_bj((t_new, SEG)),
                scratch_shapes=[pltpu.VMEM((n_rows, SEG), BF16), pltpu.VMEM((n_rows, 1), F32),
                                pltpu.VMEM((SEG, LANES), F32)]),
            out_shape=o, compiler_params=params(2), name="sb_decode_rest",
        )(page_table, live, qsb, c_top, acc_new, acc_t, g_sb, *([c_sbk] * pps + [c_sbv] * pps))

    o_sb = lax.cond(jnp.any(live == 1), walk_rest, lambda: o_top)
    return o_sb, o_df


def _gelu_tanh(x):
    return 0.5 * x * (1.0 + jnp.tanh(math.sqrt(2.0 / math.pi) * (x + 0.044715 * x * x * x)))


def _post_kernel(x_ref, msb_ref, mdf_ref, woa_ref, wob_ref, gffn_ref, wup_ref, cw_ref, cb_ref, wdn_ref,
                 gfin_ref, *rest, d_ff, fc, seq_rows):
    if seq_rows is None:
        y_ref, conv_ref, carry_ref = rest
    else:
        p1_ref, p2_ref, y_ref, conv_ref = rest
    tm = x_ref.shape[0]
    h1 = x_ref[...] + _dot(msb_ref[...].astype(BF16), woa_ref[...]) + _dot(mdf_ref[...].astype(BF16), wob_ref[...])
    h2 = _rms(h1, gffn_ref[...]).astype(BF16)
    rowi = lax.broadcasted_iota(I32, (tm, fc), 0)

    if seq_rows is None:
        @pl.when(pl.program_id(1) == 0)
        def _():
            carry_ref[...] = jnp.zeros_like(carry_ref)

    def conv(cols):
        u = _dot(h2, wup_ref[:, cols])
        r1 = pltpu.roll(u, 1, axis=0)
        r2 = pltpu.roll(u, 2, axis=0)
        if seq_rows is None:
            prev = carry_ref[:, cols]
            r1 = jnp.where(rowi == 0, prev[7:8], r1)
            r2 = jnp.where(rowi == 0, prev[6:7], jnp.where(rowi == 1, prev[7:8], r2))
            carry_ref[:, cols] = u[tm - 8:]
            conv_ref[:, cols] = u[tm - 8:]
        else:
            pos = rowi % seq_rows
            r1 = jnp.where(pos == 0, p1_ref[:, cols], r1)
            r2 = jnp.where(pos < 2, p2_ref[:, cols], r2)
            conv_ref[:, cols] = u
        cw = cw_ref[:, cols]
        return cb_ref[:, cols] + cw[2:3] * u + cw[1:2] * r1 + cw[0:1] * r2

    acc = jnp.zeros((tm, x_ref.shape[1]), F32)
    for c in range(d_ff // fc):
        gate = conv(slice(c * fc, (c + 1) * fc))
        val = conv(slice(d_ff + c * fc, d_ff + (c + 1) * fc))
        act = (_gelu_tanh(gate) * val).astype(BF16)
        acc = acc + _dot(act, wdn_ref[c * fc:(c + 1) * fc, :])
    y_ref[...] = _rms(h1 + acc, gfin_ref[...])


def _post(x, msb, mdf, woa, wob, gffn, wup, cw, cb, wdn, gfin, prev=None, *, tm, fc=1408):
    d = x.shape[-1]
    d_ff = wdn.shape[0]
    weights = (woa, wob, gffn, wup, cw, cb, wdn, gfin)
    if prev is None:
        b, t, _ = x.shape
        grid = (b, t // tm)
        act = lambda w: pl.BlockSpec((None, tm, w), lambda bi, ti: (bi, ti, 0))
        const = lambda a: pl.BlockSpec(a.shape, lambda bi, ti: (0,) * a.ndim, pipeline_mode=pl.Buffered(1))
        in_specs = [act(d), act(SEG), act(SEG)] + [const(a) for a in weights]
        out_specs = [act(d), pl.BlockSpec((None, 8, 2 * d_ff), lambda bi, ti: (bi, 0, 0))]
        out_shape = [jax.ShapeDtypeStruct((b, t, d), F32), jax.ShapeDtypeStruct((b, 8, 2 * d_ff), F32)]
        scratch = [pltpu.VMEM((8, 2 * d_ff), F32)]
        args = (x, msb, mdf) + weights
        kern = functools.partial(_post_kernel, d_ff=d_ff, fc=fc, seq_rows=None)
        name = "post_prompt"
    else:
        p1, p2, seq_rows = prev
        m = x.shape[0]
        grid = (m // tm, 1)
        act = lambda w: pl.BlockSpec((tm, w), lambda mi, ti: (mi, 0))
        const = lambda a: pl.BlockSpec(a.shape, lambda mi, ti: (0,) * a.ndim)
        in_specs = ([act(d), act(SEG), act(SEG)] + [const(a) for a in weights]
                    + [act(2 * d_ff), act(2 * d_ff)])
        out_specs = [act(d), act(2 * d_ff)]
        out_shape = [jax.ShapeDtypeStruct((m, d), F32), jax.ShapeDtypeStruct((m, 2 * d_ff), F32)]
        scratch = []
        args = (x, msb, mdf) + weights + (p1, p2)
        kern = functools.partial(_post_kernel, d_ff=d_ff, fc=fc, seq_rows=seq_rows)
        name = "post_sample"
    return pl.pallas_call(
        kern, grid=grid, in_specs=in_specs, out_specs=out_specs, out_shape=out_shape,
        scratch_shapes=scratch,
        compiler_params=pltpu.CompilerParams(dimension_semantics=("arbitrary", "arbitrary"),
                                             vmem_limit_bytes=56 * 1024 * 1024),
        name=name,
    )(*args)


def kernel(x_prompt, x_sample, cache_sb_k, cache_sb_v, cache_df_k, cache_df_v, state_conv, page_table,
           norm_mix, w_in, sb_head_norm, df_head_norm, lam_q1, lam_k1, lam_q2, lam_k2, w_o,
           norm_ffn, w_up, conv_w, conv_b, w_down, norm_final):
    depth = w_in.shape[0]
    assert depth == 1, "single-layer step"
    b, t, d = x_prompt.shape
    nb, t_new, _ = x_sample.shape
    n_pool, page = cache_sb_k.shape[1], cache_sb_k.shape[2]
    d_ff = w_down.shape[1]
    sb_heads = cache_sb_k.shape[3]
    df_heads = cache_df_k.shape[3]
    lam_init = 0.8 - 0.6 * math.exp(-0.3 * 0)
    slopes = jnp.asarray(2.0 ** (-8.0 * np.arange(1, df_heads + 1) / df_heads), dtype=F32)

    g_mix = norm_mix[0].reshape(1, d)
    w_in_b = w_in[0].astype(BF16)
    g_sb = jnp.tile(sb_head_norm[0], LANES // HEAD_DIM).reshape(1, LANES)
    g_sb_full = jnp.tile(sb_head_norm[0], SEG // HEAD_DIM).reshape(1, SEG)
    g_df = df_head_norm[0].reshape(1, LANES)
    lam_vecs = jnp.stack([lam_q1[0], lam_k1[0], lam_q2[0], lam_k2[0]]).astype(F32)
    wo_b = w_o[0].astype(BF16)
    post_w = (wo_b[:SEG], wo_b[SEG:], norm_ffn[0].reshape(1, d), w_up[0].astype(BF16), conv_w[0],
              conv_b[0].reshape(1, 2 * d_ff), w_down[0].astype(BF16), norm_final.reshape(1, d))

    (p_sbk, p_sbv, p_dfk, p_dfv, qsb, ksb, vsb, qdf, kdf, vdf) = _proj(
        x_prompt, g_mix, w_in_b, BF16, stored_layout=True, tm=PROMPT_TM)
    p_sbk = jnp.moveaxis(p_sbk.reshape(b, sb_heads, HEAD_DIM, t), 3, 1)
    p_sbv = jnp.moveaxis(p_sbv.reshape(b, sb_heads, HEAD_DIM, t), 3, 1)
    p_dfk = jnp.moveaxis(p_dfk.reshape(b, df_heads, 2, HEAD_DIM, t), 4, 1)
    o_sb, o_df = _prompt_attention(qsb, ksb, vsb, qdf, kdf, vdf, g_sb, g_df, slopes, lam_vecs, lam_init)
    y_prompt, conv_tail = _post(x_prompt, o_sb, o_df, *post_w, tm=PROMPT_TM)
    p_conv = conv_tail[:, 8 - (CONV_W - 1):]

    (s_sbk, s_sbv, s_dfk, s_dfv, sqsb, _, _, sqdf, _, _) = _proj(
        x_sample.reshape(1, nb * t_new, d), g_mix, w_in_b, F32, stored_layout=False, tm=nb * t_new)
    s3 = lambda a: a.reshape(nb, t_new, SEG)
    pos_minor = lambda c: jnp.moveaxis(c.reshape(n_pool, page, SEG), 1, 2)
    so_sb, so_df = _decode_attention(
        page_table, slopes, lam_vecs, s3(sqsb), s3(sqdf), s3(s_sbk), s3(s_sbv), s3(s_dfk), s3(s_dfv),
        g_sb_full, g_df, pos_minor(cache_sb_k), pos_minor(cache_sb_v), pos_minor(cache_df_k),
        cache_df_v.reshape(n_pool, page * df_heads, 2 * HEAD_DIM), lam_init)
    st = state_conv[0]
    zeros = jnp.zeros((nb, t_new - 1, 2 * d_ff), F32)
    prev1 = jnp.concatenate([st[:, 1:2], zeros], axis=1).reshape(nb * t_new, 2 * d_ff)
    prev2 = jnp.concatenate([st, zeros[:, 1:]], axis=1).reshape(nb * t_new, 2 * d_ff)
    y_sample, u_s = _post(x_sample.reshape(nb * t_new, d), so_sb.reshape(nb * t_new, SEG),
                          so_df.reshape(nb * t_new, SEG), *post_w, prev=(prev1, prev2, t_new),
                          tm=nb * t_new)
    s_conv = u_s.reshape(nb, t_new, 2 * d_ff)[:, t_new - (CONV_W - 1):]

    return (y_prompt, y_sample.reshape(nb, t_new, d),
            p_sbk[None], p_sbv[None], p_dfk[None], p_dfv.reshape(1, b, t, df_heads, 2 * HEAD_DIM),
            p_conv[None],
            s_sbk.reshape(1, nb, t_new, sb_heads, HEAD_DIM), s_sbv.reshape(1, nb, t_new, sb_heads, HEAD_DIM),
            s_dfk.reshape(1, nb, t_new, df_heads, 2, HEAD_DIM), s_dfv.reshape(1, nb, t_new, df_heads, 2 * HEAD_DIM),
            s_conv[None])
```

```python
import functools
import math

import numpy as np
import jax
import jax.numpy as jnp
from jax import lax
from jax.experimental import pallas as pl
from jax.experimental.pallas import tpu as pltpu

F32 = jnp.float32
BF16 = jnp.bfloat16
I32 = jnp.int32

EPS = 1e-6
HEAD_DIM = 64
LANES = 128
SEG = 512
N_SEG = 6
CONV_W = 3
NEG_INF = float("-inf")
VMEM_LIMIT = 48 * 1024 * 1024
PROMPT_TM = 512
F32_EXP_ZERO = -104.0


def _rms(x, g):
    ms = jnp.mean(x * x, axis=-1, keepdims=True)
    return x * lax.rsqrt(ms + EPS) * g


def _nt_dot(a, b):
    return lax.dot_general(a, b, (((1,), (1,)), ((), ())), preferred_element_type=F32)


def _dot(a, b):
    return jnp.dot(a, b, preferred_element_type=F32)


def _cumsum_matrix(tk):
    j = lax.broadcasted_iota(I32, (2 * tk, tk), 0)
    s = lax.broadcasted_iota(I32, (2 * tk, tk), 1)
    j = jnp.where(j >= tk, j - tk, j)
    return jnp.where(j >= s, 1.0, 0.0).astype(BF16)


def _rev_cumsum(l, uu):
    hi = l.astype(BF16)
    lo = (l - hi.astype(F32)).astype(BF16)
    return _dot(jnp.concatenate([hi, lo], axis=1), uu)


def _log1m_beta(z):
    return -(jnp.maximum(z, 0.0) + jnp.log(1.0 + jnp.exp(-jnp.abs(z))))


def _proj_kernel(x_ref, g_ref, w_ref, sbk_ref, sbv_ref, dfk_ref, dfv_ref,
                 qsb_ref, ksb_ref, vsb_ref, qdf_ref, kdf_ref, vdf_ref, *, stored_layout):
    h = _rms(x_ref[...], g_ref[...]).astype(BF16)
    tm = h.shape[0]

    def seg(j):
        return _dot(h, w_ref[:, j * SEG:(j + 1) * SEG])

    def put(ref, val):
        ref[...] = val.T if stored_layout else val

    scale = HEAD_DIM ** -0.5
    qsb_ref[...] = (seg(0) * scale).astype(qsb_ref.dtype)
    k = seg(1)
    put(sbk_ref, k)
    ksb_ref[...] = k.astype(BF16)
    v = seg(2)
    put(sbv_ref, v)
    vsb_ref[...] = v.astype(BF16)
    qdf_ref[...] = (seg(3) * scale).astype(qdf_ref.dtype)
    k = seg(4)
    put(dfk_ref, k)
    kdf_ref[...] = k.astype(BF16)
    v = seg(5)
    if stored_layout:
        n_heads = SEG // LANES
        for hd in range(n_heads):
            dfv_ref[pl.ds(hd, tm, stride=n_heads), :] = v[:, hd * LANES:(hd + 1) * LANES]
    else:
        dfv_ref[...] = v
    vdf_ref[...] = v.astype(BF16)


def _proj(x, g, w, q_dtype, stored_layout, tm):
    b, t, d = x.shape
    grid = (b, t // tm)
    rows = pl.BlockSpec((None, tm, SEG), lambda bi, ti: (bi, ti, 0))
    const = lambda bi, ti: (0, 0)
    f32o = jax.ShapeDtypeStruct((b, t, SEG), F32)
    b16o = jax.ShapeDtypeStruct((b, t, SEG), BF16)
    qo = jax.ShapeDtypeStruct((b, t, SEG), q_dtype)
    if stored_layout:
        n_heads = SEG // LANES
        tspec = pl.BlockSpec((None, SEG, tm), lambda bi, ti: (bi, 0, ti))
        to = jax.ShapeDtypeStruct((b, SEG, t), F32)
        kv_specs = [tspec, tspec, tspec, pl.BlockSpec((None, tm * n_heads, LANES), lambda bi, ti: (bi, ti, 0))]
        kv_shapes = [to, to, to, jax.ShapeDtypeStruct((b, t * n_heads, LANES), F32)]
    else:
        kv_specs = [rows] * 4
        kv_shapes = [f32o] * 4
    return pl.pallas_call(
        functools.partial(_proj_kernel, stored_layout=stored_layout),
        grid=grid,
        in_specs=[pl.BlockSpec((None, tm, d), lambda bi, ti: (bi, ti, 0)), pl.BlockSpec((1, d), const),
                  pl.BlockSpec((d, N_SEG * SEG), const, pipeline_mode=pl.Buffered(1))],
        out_specs=kv_specs + [rows] * 6,
        out_shape=kv_shapes + [qo, b16o, b16o, qo, b16o, b16o],
        compiler_params=pltpu.CompilerParams(dimension_semantics=("arbitrary", "arbitrary"),
                                             vmem_limit_bytes=VMEM_LIMIT),
        name="proj",
    )(x, g, w)


def _split_lane_halves_rows(x):
    lo = lax.broadcasted_iota(I32, x.shape, 1) < HEAD_DIM
    zero = jnp.zeros_like(x)
    return jnp.concatenate([jnp.where(lo, x, zero), jnp.where(lo, zero, x)], axis=0)


def _sb_prompt_kernel(q_ref, k_ref, v_ref, g_ref, o_ref, q2_ref, c_ref, acc_ref, *, tq, tk):
    qi = pl.program_id(1)
    n_blk = SEG // LANES
    for p in range(n_blk):
        q2_ref[p] = _split_lane_halves_rows(q_ref[:, p * LANES:(p + 1) * LANES])
    c_ref[...] = jnp.zeros_like(c_ref)
    acc_ref[...] = jnp.zeros_like(acc_ref)
    uu = _cumsum_matrix(tk)
    row = lax.broadcasted_iota(I32, (2 * tq, tk), 0)
    col = lax.broadcasted_iota(I32, (2 * tq, tk), 1)
    qpos = qi * tq + jnp.where(row >= tq, row - tq, row)

    def block(kj, masked):
        start = pl.multiple_of(kj * tk, tk)
        if masked:
            strict = start + col < qpos
        for p in range(n_blk):
            lanes = slice(p * LANES, (p + 1) * LANES)
            z = _nt_dot(q2_ref[p], k_ref[pl.ds(start, tk), lanes])
            l = _log1m_beta(z)
            if masked:
                l = jnp.where(strict, l, 0.0)
            cinc = _rev_cumsum(l, uu)
            c = c_ref[p]
            a = jnp.exp(z + c + cinc)
            if masked:
                a = jnp.where(strict, a, 0.0)
            c_ref[p] = c + cinc[:, 0:1]
            ab = a.astype(BF16)
            a_cat = jnp.concatenate([ab[:tq], ab[tq:]], axis=1)
            acc_ref[p] += _dot(a_cat, _split_lane_halves_rows(v_ref[pl.ds(start, tk), lanes]))

    n_diag = tq // tk
    top = (qi + 1) * n_diag - 1
    for d in range(n_diag):
        block(top - d, True)

    def c_max():
        c = c_ref[0]
        for p in range(1, n_blk):
            c = jnp.maximum(c, c_ref[p])
        return jnp.max(c)

    def body(carry):
        kj, _ = carry
        block(kj, False)
        return kj - 1, c_max()

    lax.while_loop(lambda carry: (carry[0] >= 0) & (carry[1] > F32_EXP_ZERO), body,
                   (top - n_diag, c_max()))

    lo = lax.broadcasted_iota(I32, (tq, LANES), 1) < HEAD_DIM
    for p in range(n_blk):
        o = acc_ref[p]
        o2 = o * o
        s_lo = jnp.sum(jnp.where(lo, o2, 0.0), axis=-1, keepdims=True)
        s_hi = jnp.sum(jnp.where(lo, 0.0, o2), axis=-1, keepdims=True)
        ms = jnp.where(lo, s_lo, s_hi) * (1.0 / HEAD_DIM)
        o_ref[:, p * LANES:(p + 1) * LANES] = (o * lax.rsqrt(ms + EPS) * g_ref[...]).astype(o_ref.dtype)


def _lam_value(lam_ref, lam_init):
    lv = lam_ref[...]
    s1 = jnp.sum(lv[0:1] * lv[1:2], axis=-1, keepdims=True)
    s2 = jnp.sum(lv[2:3] * lv[3:4], axis=-1, keepdims=True)
    return jnp.exp(s1) - jnp.exp(s2) + lam_init


def _df_prompt_kernel(slopes_ref, lam_ref, q_ref, k_ref, v_ref, g_ref, o_ref, q2_ref, m_ref, acc_ref,
                      *, tq, tk, lam_init):
    qi = pl.program_id(1)
    n_heads = SEG // LANES
    for h in range(n_heads):
        q2_ref[h] = _split_lane_halves_rows(q_ref[:, h * LANES:(h + 1) * LANES])
    m_ref[...] = jnp.full_like(m_ref, NEG_INF)
    acc_ref[...] = jnp.zeros_like(acc_ref)

    def block(kj, width, masked):
        start = pl.multiple_of(kj * tk, tk)
        row = lax.broadcasted_iota(I32, (2 * tq, width), 0)
        col = lax.broadcasted_iota(I32, (2 * tq, width), 1)
        dist_i = jnp.where(row >= tq, row - tq, row) - col + (qi * tq - start)
        dist = dist_i.astype(F32)
        ones = jnp.ones((width, LANES), BF16)
        for h in range(n_heads):
            lanes = slice(h * LANES, (h + 1) * LANES)
            s = _nt_dot(q2_ref[h], k_ref[pl.ds(start, width), lanes]) - slopes_ref[h] * dist
            if masked:
                s = jnp.where(dist_i >= 0, s, NEG_INF)
            m_i = m_ref[h]
            m_new = jnp.maximum(m_i, jnp.max(s, axis=-1, keepdims=True))
            alpha = jnp.exp(m_i - m_new)
            p = jnp.exp(s - m_new).astype(BF16)
            v_aug = jnp.concatenate([v_ref[pl.ds(start, width), lanes], ones], axis=1)
            acc_ref[h] = alpha * acc_ref[h] + _dot(p, v_aug)
            m_ref[h] = m_new

    top = ((qi + 1) * tq - 1) // tk
    n_sub = tk // tq
    for r in range(n_sub):
        pl.when(qi % n_sub == r)(functools.partial(block, top, (r + 1) * tq, True))
    lax.fori_loop(0, top, lambda i, _: block(top - 1 - i, tk, False), None)

    lam = _lam_value(lam_ref, lam_init)
    for h in range(n_heads):
        acc = acc_ref[h]
        r = acc[:, :LANES] / acc[:, LANES:]
        o = r[:tq] - lam * r[tq:]
        o_ref[:, h * LANES:(h + 1) * LANES] = (_rms(o, g_ref[...]) * (1.0 - lam_init)).astype(o_ref.dtype)


def _prompt_attention(qsb, ksb, vsb, qdf, kdf, vdf, g_sb, g_df, slopes, lam_vecs, lam_init,
                      tq_sb=256, tk_sb=256, tq_df=256, tk_df=1024):
    b, t, _ = qsb.shape
    n_blk = SEG // LANES
    qspec = lambda tq: pl.BlockSpec((None, tq, SEG), lambda bi, qi: (bi, qi, 0))
    kvspec = pl.BlockSpec((None, t, SEG), lambda bi, qi: (bi, 0, 0))
    gspec = pl.BlockSpec((1, LANES), lambda bi, qi: (0, 0))
    out = jax.ShapeDtypeStruct((b, t, SEG), BF16)
    params = pltpu.CompilerParams(dimension_semantics=("arbitrary",) * 2, vmem_limit_bytes=VMEM_LIMIT)
    q2_scratch = lambda tq: pltpu.VMEM((n_blk, 2 * tq, LANES), BF16)
    o_sb = pl.pallas_call(
        functools.partial(_sb_prompt_kernel, tq=tq_sb, tk=tk_sb),
        grid=(b, t // tq_sb), in_specs=[qspec(tq_sb), kvspec, kvspec, gspec], out_specs=qspec(tq_sb),
        out_shape=out,
        scratch_shapes=[q2_scratch(tq_sb), pltpu.VMEM((n_blk, 2 * tq_sb, 1), F32),
                        pltpu.VMEM((n_blk, tq_sb, LANES), F32)],
        compiler_params=params, name="sb_prompt",
    )(qsb, ksb, vsb, g_sb)
    o_df = pl.pallas_call(
        functools.partial(_df_prompt_kernel, tq=tq_df, tk=tk_df, lam_init=lam_init),
        grid=(b, t // tq_df),
        in_specs=[pl.BlockSpec(memory_space=pltpu.SMEM),
                  pl.BlockSpec((4, HEAD_DIM), lambda bi, qi: (0, 0)),
                  qspec(tq_df), kvspec, kvspec, gspec],
        out_specs=qspec(tq_df), out_shape=out,
        scratch_shapes=[q2_scratch(tq_df), pltpu.VMEM((n_blk, 2 * tq_df, 1), F32),
                        pltpu.VMEM((n_blk, 2 * tq_df, 2 * LANES), F32)],
        compiler_params=params, name="df_prompt",
    )(slopes, lam_vecs, qdf, kdf, vdf, g_df)
    return o_sb, o_df


def _replicated_queries(q, t_new):
    n_rows = SEG // HEAD_DIM * t_new
    lane_chunk = lax.broadcasted_iota(I32, (n_rows, SEG), 1) // HEAD_DIM
    own = lane_chunk == lax.broadcasted_iota(I32, (n_rows, SEG), 0) // t_new
    return jnp.where(own, jnp.concatenate([q] * (n_rows // t_new), axis=0), 0.0).astype(BF16)


def _pad_rows(x, page):
    return jnp.concatenate([x, jnp.zeros((page - x.shape[0], SEG), F32)], axis=0).astype(BF16)


def _sb_pages(qs, k_refs, v_refs, c, acc_t, uu):
    n_rows, page = qs.shape[0], uu.shape[1]
    z = [_dot(qs, k[...].astype(BF16)) for k in k_refs]
    cinc = _rev_cumsum(jnp.concatenate([_log1m_beta(zu) for zu in z], axis=0), uu)
    pad = jnp.zeros((LANES - n_rows, page), BF16)
    for u, v in enumerate(v_refs):
        cu = cinc[u * n_rows:(u + 1) * n_rows]
        a = jnp.exp(z[u] + c + cu).astype(BF16)
        c = c + cu[:, 0:1]
        acc_t = acc_t + _nt_dot(v[...].astype(BF16), jnp.concatenate([a, pad], axis=0))
    return c, acc_t


def _sb_head_outputs(acc, g, t_new):
    lane = lax.broadcasted_iota(I32, (t_new, SEG), 1)
    o = jnp.zeros((t_new, SEG), F32)
    for h in range(SEG // HEAD_DIM):
        o = o + jnp.where(lane // HEAD_DIM == h, acc[h * t_new:(h + 1) * t_new], 0.0)
    o2 = o * o
    ms = jnp.zeros((t_new, SEG), F32)
    for h in range(SEG // HEAD_DIM):
        sel = lane // HEAD_DIM == h
        ms = jnp.where(sel, jnp.sum(jnp.where(sel, o2, 0.0), axis=-1, keepdims=True), ms)
    return o * lax.rsqrt(ms * (1.0 / HEAD_DIM) + EPS) * g


def _sb_decode_top_kernel(pt_ref, q_ref, nk_ref, nv_ref, g_ref, *rest, pps, page, t_new):
    ck, cv = rest[:pps], rest[pps:2 * pps]
    c_out, acc_out, acct_out, o_ref = rest[2 * pps:]
    n_rows = SEG // HEAD_DIM * t_new
    uu = _cumsum_matrix(page)
    qs = _replicated_queries(q_ref[...], t_new)
    i_q = lax.broadcasted_iota(I32, (n_rows, 1), 0) % t_new
    strict = lax.broadcasted_iota(I32, (n_rows, page), 1) < i_q
    z = _nt_dot(qs, _pad_rows(nk_ref[...], page))
    cinc = _rev_cumsum(jnp.where(strict, _log1m_beta(z), 0.0), uu)
    a = jnp.where(strict, jnp.exp(z + cinc), 0.0).astype(BF16)
    acc_new = _dot(a, _pad_rows(nv_ref[...], page))
    c, acc_t = _sb_pages(qs, ck, cv, cinc[:, 0:1], jnp.zeros((SEG, LANES), F32), uu)
    c_out[...] = jnp.broadcast_to(c, (n_rows, LANES))
    acc_out[...] = acc_new
    acct_out[...] = acc_t
    o_ref[...] = _sb_head_outputs(acc_new + acc_t.T[:n_rows], g_ref[...], t_new)


def _sb_decode_rest_kernel(pt_ref, live_ref, q_ref, c_in, acc_in, acct_in, g_ref, *rest, pps, page, t_new):
    ck, cv = rest[:pps], rest[pps:2 * pps]
    o_ref, qr_ref, c_ref, acct_ref = rest[2 * pps:]
    b, j = pl.program_id(0), pl.program_id(1)
    n_rows = SEG // HEAD_DIM * t_new

    @pl.when(j == 0)
    def _():
        qr_ref[...] = _replicated_queries(q_ref[...], t_new)
        c_ref[...] = c_in[:, 0:1]
        acct_ref[...] = acct_in[...]

    @pl.when(live_ref[b] == 1)
    def _():
        c, acc_t = _sb_pages(qr_ref[...], ck, cv, c_ref[...], acct_ref[...], _cumsum_matrix(page))
        c_ref[...] = c
        acct_ref[...] = acc_t

    @pl.when(j == pl.num_programs(1) - 1)
    def _():
        o_ref[...] = _sb_head_outputs(acc_in[...] + acct_ref[...].T[:n_rows], g_ref[...], t_new)


def _df_decode_kernel(pt_ref, slopes_ref, lam_ref, q_ref, nk_ref, nv_ref, g_ref, *rest,
                      n_pages, pps, page, t_new, lam_init):
    ck, cv = rest[:pps], rest[pps:2 * pps]
    o_ref, qr_ref, m_ref, l_ref, acc_ref = rest[2 * pps:]
    j = pl.program_id(1)
    n_heads = SEG // LANES
    n_rows = SEG // HEAD_DIM * t_new
    past_len = n_pages * page
    rowq = lax.broadcasted_iota(I32, (n_rows, 1), 0)
    i_q = rowq % t_new
    head_of_row = rowq // (2 * t_new)
    slope = jnp.zeros((n_rows, 1), F32)
    for h in range(n_heads):
        slope = jnp.where(head_of_row == h, slopes_ref[h], slope)
    col = lax.broadcasted_iota(I32, (n_rows, page), 1)

    @pl.when(j == 0)
    def _():
        qr_ref[...] = _replicated_queries(q_ref[...], t_new)
        s = _nt_dot(qr_ref[...], _pad_rows(nk_ref[...], page)) - slope * (i_q - col).astype(F32)
        s = jnp.where(col <= i_q, s, NEG_INF)
        m_new = jnp.max(s, axis=-1, keepdims=True)
        p = jnp.exp(s - m_new)
        m_ref[...] = m_new
        l_ref[...] = jnp.sum(p, axis=-1, keepdims=True)
        pv = _dot(p.astype(BF16), _pad_rows(nv_ref[...], page))
        acc = jnp.zeros((n_rows, LANES), F32)
        for h in range(n_heads):
            acc = jnp.where(head_of_row == h, pv[:, h * LANES:(h + 1) * LANES], acc)
        acc_ref[...] = acc

    qd = qr_ref[...]
    s = []
    for u in range(pps):
        lp = n_pages - 1 - (j * pps + u)
        dist = (past_len + i_q - lp * page - col).astype(F32)
        s.append(_dot(qd, ck[u][...].astype(BF16)) - slope * dist)
    smax = s[0]
    for u in range(1, pps):
        smax = jnp.maximum(smax, s[u])
    m_i = m_ref[...]
    m_new = jnp.maximum(m_i, jnp.max(smax, axis=-1, keepdims=True))
    alpha = jnp.exp(m_i - m_new)
    psum = jnp.zeros((n_rows, page), F32)
    acc = alpha * acc_ref[...]
    for u in range(pps):
        p = jnp.exp(s[u] - m_new)
        psum = psum + p
        p_bd = jnp.concatenate([jnp.where(head_of_row == h, p, 0.0) for h in range(n_heads)], axis=1)
        v_hm = jnp.concatenate([cv[u][pl.ds(h, page, stride=n_heads), :] for h in range(n_heads)], axis=0)
        acc = acc + _dot(p_bd.astype(BF16), v_hm.astype(BF16))
    l_ref[...] = alpha * l_ref[...] + jnp.sum(psum, axis=-1, keepdims=True)
    acc_ref[...] = acc
    m_ref[...] = m_new

    @pl.when(j == pl.num_programs(1) - 1)
    def _():
        lam = _lam_value(lam_ref, lam_init)
        acc = acc_ref[...]
        l_i = l_ref[...]
        outs = []
        for h in range(n_heads):
            r0 = 2 * h * t_new
            o0 = acc[r0:r0 + t_new] / l_i[r0:r0 + t_new]
            o1 = acc[r0 + t_new:r0 + 2 * t_new] / l_i[r0 + t_new:r0 + 2 * t_new]
            outs.append(_rms(o0 - lam * o1, g_ref[...]) * (1.0 - lam_init))
        o_ref[...] = jnp.concatenate(outs, axis=1)


def _decode_attention(page_table, slopes, lam_vecs, qsb, qdf, nsbk, nsbv, ndfk, ndfv, g_sb, g_df,
                      c_sbk, c_sbv, c_dfk, c_dfv, lam_init, pages_per_step=32, top_pages=4,
                      rest_pages_per_step=31):
    nb, t_new, _ = qsb.shape
    n_pages = page_table.shape[1]
    page = c_sbk.shape[2]
    pps = pages_per_step
    n_rows = SEG // HEAD_DIM * t_new
    slab = (None, SEG, page)
    o = jax.ShapeDtypeStruct((nb, t_new, SEG), F32)
    params = lambda n: pltpu.CompilerParams(dimension_semantics=("arbitrary",) * n, vmem_limit_bytes=VMEM_LIMIT)

    per_b = pl.BlockSpec((None, t_new, SEG), lambda b, j, pt: (b, 0, 0))
    const2 = lambda shape: pl.BlockSpec(shape, lambda b, j, pt: (0, 0))
    df_pages = [pl.BlockSpec(slab, functools.partial(
        lambda b, j, pt, u: (pt[b, n_pages - 1 - (j * pps + u)], 0, 0), u=u)) for u in range(pps)]
    o_df = pl.pallas_call(
        functools.partial(_df_decode_kernel, n_pages=n_pages, pps=pps, page=page, t_new=t_new,
                          lam_init=lam_init),
        grid_spec=pltpu.PrefetchScalarGridSpec(
            num_scalar_prefetch=1, grid=(nb, n_pages // pps),
            in_specs=[pl.BlockSpec(memory_space=pltpu.SMEM), const2((4, HEAD_DIM)), per_b, per_b, per_b,
                      const2((1, LANES))] + df_pages * 2,
            out_specs=per_b,
            scratch_shapes=[pltpu.VMEM((n_rows, SEG), BF16), pltpu.VMEM((n_rows, 1), F32),
                            pltpu.VMEM((n_rows, 1), F32), pltpu.VMEM((n_rows, LANES), F32)]),
        out_shape=o, compiler_params=params(2), name="df_decode",
    )(page_table, slopes, lam_vecs, qdf, ndfk, ndfv, g_df, *([c_dfk] * pps + [c_dfv] * pps))

    pps = top_pages
    row_b = lambda shape: pl.BlockSpec((None,) + shape, lambda b, pt: (b, 0, 0))
    top_specs = [pl.BlockSpec(slab, functools.partial(
        lambda b, pt, u: (pt[b, n_pages - 1 - u], 0, 0), u=u)) for u in range(pps)]
    state_shapes = [jax.ShapeDtypeStruct((nb, n_rows, LANES), F32), jax.ShapeDtypeStruct((nb, n_rows, SEG), F32),
                    jax.ShapeDtypeStruct((nb, SEG, LANES), F32), o]
    c_top, acc_new, acc_t, o_top = pl.pallas_call(
        functools.partial(_sb_decode_top_kernel, pps=pps, page=page, t_new=t_new),
        grid_spec=pltpu.PrefetchScalarGridSpec(
            num_scalar_prefetch=1, grid=(nb,),
            in_specs=[row_b((t_new, SEG))] * 3 + [pl.BlockSpec((1, SEG), lambda b, pt: (0, 0))] + top_specs * 2,
            out_specs=[row_b((n_rows, LANES)), row_b((n_rows, SEG)), row_b((SEG, LANES)),
                       row_b((t_new, SEG))]),
        out_shape=state_shapes, compiler_params=params(1), name="sb_decode_top",
    )(page_table, qsb, nsbk, nsbv, g_sb, *([c_sbk] * pps + [c_sbv] * pps))

    live = (jnp.max(c_top, axis=(1, 2)) > F32_EXP_ZERO).astype(I32)
    n_rest = n_pages - top_pages
    pps = max(p for p in range(1, rest_pages_per_step + 1) if n_rest % p == 0)

    def rest_index(b, j, pt, live, u):
        lp = n_rest - 1 - (j * pps + u)
        return (jnp.where(live[b] == 1, pt[b, lp], pt[0, 0]), 0, 0)

    rest_specs = [pl.BlockSpec(slab, functools.partial(rest_index, u=u)) for u in range(pps)]
    row_bj = lambda shape: pl.BlockSpec((None,) + shape, lambda b, j, pt, live: (b, 0, 0))

    def walk_rest():
        return pl.pallas_call(
            functools.partial(_sb_decode_rest_kernel, pps=pps, page=page, t_new=t_new),
            grid_spec=pltpu.PrefetchScalarGridSpec(
                num_scalar_prefetch=2, grid=(nb, n_rest // pps),
                in_specs=[row_bj((t_new, SEG)), row_bj((n_rows, LANES)), row_bj((n_rows, SEG)),
                          row_bj((SEG, LANES)), pl.BlockSpec((1, SEG), lambda b, j, pt, live: (0, 0))]
                         + rest_specs * 2,
                out_specs=row_bj((t_new, SEG)),
                scratch_shapes=[pltpu.VMEM((n_rows, SEG), BF16), pltpu.VMEM((n_rows, 1), F32),
                                pltpu.VMEM((SEG, LANES), F32)]),
            out_shape=o, compiler_params=params(2), name="sb_decode_rest",
        )(page_table, live, qsb, c_top, acc_new, acc_t, g_sb, *([c_sbk] * pps + [c_sbv] * pps))

    o_sb = lax.cond(jnp.any(live == 1), walk_rest, lambda: o_top)
    return o_sb, o_df


def _gelu_tanh(x):
    return 0.5 * x * (1.0 + jnp.tanh(math.sqrt(2.0 / math.pi) * (x + 0.044715 * x * x * x)))


def _post_kernel(x_ref, msb_ref, mdf_ref, woa_ref, wob_ref, gffn_ref, wup_ref, cw_ref, cb_ref, wdn_ref,
                 gfin_ref, *rest, d_ff, fc, seq_rows):
    if seq_rows is None:
        y_ref, conv_ref, carry_ref = rest
    else:
        p1_ref, p2_ref, y_ref, conv_ref = rest
    tm = x_ref.shape[0]
    h1 = x_ref[...] + _dot(msb_ref[...].astype(BF16), woa_ref[...]) + _dot(mdf_ref[...].astype(BF16), wob_ref[...])
    h2 = _rms(h1, gffn_ref[...]).astype(BF16)
    rowi = lax.broadcasted_iota(I32, (tm, fc), 0)

    if seq_rows is None:
        @pl.when(pl.program_id(1) == 0)
        def _():
            carry_ref[...] = jnp.zeros_like(carry_ref)

    def conv(cols):
        u = _dot(h2, wup_ref[:, cols])
        r1 = pltpu.roll(u, 1, axis=0)
        r2 = pltpu.roll(u, 2, axis=0)
        if seq_rows is None:
            prev = carry_ref[:, cols]
            r1 = jnp.where(rowi == 0, prev[7:8], r1)
            r2 = jnp.where(rowi == 0, prev[6:7], jnp.where(rowi == 1, prev[7:8], r2))
            carry_ref[:, cols] = u[tm - 8:]
            conv_ref[:, cols] = u[tm - 8:]
        else:
            pos = rowi % seq_rows
            r1 = jnp.where(pos == 0, p1_ref[:, cols], r1)
            r2 = jnp.where(pos < 2, p2_ref[:, cols], r2)
            conv_ref[:, cols] = u
        cw = cw_ref[:, cols]
        return cb_ref[:, cols] + cw[2:3] * u + cw[1:2] * r1 + cw[0:1] * r2

    acc = jnp.zeros((tm, x_ref.shape[1]), F32)
    for c in range(d_ff // fc):
        gate = conv(slice(c * fc, (c + 1) * fc))
        val = conv(slice(d_ff + c * fc, d_ff + (c + 1) * fc))
        act = (_gelu_tanh(gate) * val).astype(BF16)
        acc = acc + _dot(act, wdn_ref[c * fc:(c + 1) * fc, :])
    y_ref[...] = _rms(h1 + acc, gfin_ref[...])


def _post(x, msb, mdf, woa, wob, gffn, wup, cw, cb, wdn, gfin, prev=None, *, tm, fc=2816):
    d = x.shape[-1]
    d_ff = wdn.shape[0]
    weights = (woa, wob, gffn, wup, cw, cb, wdn, gfin)
    if prev is None:
        b, t, _ = x.shape
        grid = (b, t // tm)
        act = lambda w: pl.BlockSpec((None, tm, w), lambda bi, ti: (bi, ti, 0))
        const = lambda a: pl.BlockSpec(a.shape, lambda bi, ti: (0,) * a.ndim, pipeline_mode=pl.Buffered(1))
        in_specs = [act(d), act(SEG), act(SEG)] + [const(a) for a in weights]
        out_specs = [act(d), pl.BlockSpec((None, 8, 2 * d_ff), lambda bi, ti: (bi, 0, 0))]
        out_shape = [jax.ShapeDtypeStruct((b, t, d), F32), jax.ShapeDtypeStruct((b, 8, 2 * d_ff), F32)]
        scratch = [pltpu.VMEM((8, 2 * d_ff), F32)]
        args = (x, msb, mdf) + weights
        kern = functools.partial(_post_kernel, d_ff=d_ff, fc=fc, seq_rows=None)
        name = "post_prompt"
    else:
        p1, p2, seq_rows = prev
        m = x.shape[0]
        grid = (m // tm, 1)
        act = lambda w: pl.BlockSpec((tm, w), lambda mi, ti: (mi, 0))
        const = lambda a: pl.BlockSpec(a.shape, lambda mi, ti: (0,) * a.ndim)
        in_specs = ([act(d), act(SEG), act(SEG)] + [const(a) for a in weights]
                    + [act(2 * d_ff), act(2 * d_ff)])
        out_specs = [act(d), act(2 * d_ff)]
        out_shape = [jax.ShapeDtypeStruct((m, d), F32), jax.ShapeDtypeStruct((m, 2 * d_ff), F32)]
        scratch = []
        args = (x, msb, mdf) + weights + (p1, p2)
        kern = functools.partial(_post_kernel, d_ff=d_ff, fc=fc, seq_rows=seq_rows)
        name = "post_sample"
    return pl.pallas_call(
        kern, grid=grid, in_specs=in_specs, out_specs=out_specs, out_shape=out_shape,
        scratch_shapes=scratch,
        compiler_params=pltpu.CompilerParams(dimension_semantics=("arbitrary", "arbitrary"),
                                             vmem_limit_bytes=56 * 1024 * 1024),
        name=name,
    )(*args)


def kernel(x_prompt, x_sample, cache_sb_k, cache_sb_v, cache_df_k, cache_df_v, state_conv, page_table,
           norm_mix, w_in, sb_head_norm, df_head_norm, lam_q1, lam_k1, lam_q2, lam_k2, w_o,
           norm_ffn, w_up, conv_w, conv_b, w_down, norm_final):
    depth = w_in.shape[0]
    assert depth == 1, "single-layer step"
    b, t, d = x_prompt.shape
    nb, t_new, _ = x_sample.shape
    n_pool, page = cache_sb_k.shape[1], cache_sb_k.shape[2]
    d_ff = w_down.shape[1]
    sb_heads = cache_sb_k.shape[3]
    df_heads = cache_df_k.shape[3]
    lam_init = 0.8 - 0.6 * math.exp(-0.3 * 0)
    slopes = jnp.asarray(2.0 ** (-8.0 * np.arange(1, df_heads + 1) / df_heads), dtype=F32)

    g_mix = norm_mix[0].reshape(1, d)
    w_in_b = w_in[0].astype(BF16)
    g_sb = jnp.tile(sb_head_norm[0], LANES // HEAD_DIM).reshape(1, LANES)
    g_sb_full = jnp.tile(sb_head_norm[0], SEG // HEAD_DIM).reshape(1, SEG)
    g_df = df_head_norm[0].reshape(1, LANES)
    lam_vecs = jnp.stack([lam_q1[0], lam_k1[0], lam_q2[0], lam_k2[0]]).astype(F32)
    wo_b = w_o[0].astype(BF16)
    post_w = (wo_b[:SEG], wo_b[SEG:], norm_ffn[0].reshape(1, d), w_up[0].astype(BF16), conv_w[0],
              conv_b[0].reshape(1, 2 * d_ff), w_down[0].astype(BF16), norm_final.reshape(1, d))

    (p_sbk, p_sbv, p_dfk, p_dfv, qsb, ksb, vsb, qdf, kdf, vdf) = _proj(
        x_prompt, g_mix, w_in_b, BF16, stored_layout=True, tm=PROMPT_TM)
    p_sbk = jnp.moveaxis(p_sbk.reshape(b, sb_heads, HEAD_DIM, t), 3, 1)
    p_sbv = jnp.moveaxis(p_sbv.reshape(b, sb_heads, HEAD_DIM, t), 3, 1)
    p_dfk = jnp.moveaxis(p_dfk.reshape(b, df_heads, 2, HEAD_DIM, t), 4, 1)
    o_sb, o_df = _prompt_attention(qsb, ksb, vsb, qdf, kdf, vdf, g_sb, g_df, slopes, lam_vecs, lam_init)
    y_prompt, conv_tail = _post(x_prompt, o_sb, o_df, *post_w, tm=PROMPT_TM)
    p_conv = conv_tail[:, 8 - (CONV_W - 1):]

    (s_sbk, s_sbv, s_dfk, s_dfv, sqsb, _, _, sqdf, _, _) = _proj(
        x_sample.reshape(1, nb * t_new, d), g_mix, w_in_b, F32, stored_layout=False, tm=nb * t_new)
    s3 = lambda a: a.reshape(nb, t_new, SEG)
    pos_minor = lambda c: jnp.moveaxis(c.reshape(n_pool, page, SEG), 1, 2)
    so_sb, so_df = _decode_attention(
        page_table, slopes, lam_vecs, s3(sqsb), s3(sqdf), s3(s_sbk), s3(s_sbv), s3(s_dfk), s3(s_dfv),
        g_sb_full, g_df, pos_minor(cache_sb_k), pos_minor(cache_sb_v), pos_minor(cache_df_k),
        cache_df_v.reshape(n_pool, page * df_heads, 2 * HEAD_DIM), lam_init)
    st = state_conv[0]
    zeros = jnp.zeros((nb, t_new - 1, 2 * d_ff), F32)
    prev1 = jnp.concatenate([st[:, 1:2], zeros], axis=1).reshape(nb * t_new, 2 * d_ff)
    prev2 = jnp.concatenate([st, zeros[:, 1:]], axis=1).reshape(nb * t_new, 2 * d_ff)
    y_sample, u_s = _post(x_sample.reshape(nb * t_new, d), so_sb.reshape(nb * t_new, SEG),
                          so_df.reshape(nb * t_new, SEG), *post_w, prev=(prev1, prev2, t_new),
                          tm=nb * t_new)
    s_conv = u_s.reshape(nb, t_new, 2 * d_ff)[:, t_new - (CONV_W - 1):]

    return (y_prompt, y_sample.reshape(nb, t_new, d),
            p_sbk[None], p_sbv[None], p_dfk[None], p_dfv.reshape(1, b, t, df_heads, 2 * HEAD_DIM),
            p_conv[None],
            s_sbk.reshape(1, nb, t_new, sb_heads, HEAD_DIM), s_sbv.reshape(1, nb, t_new, sb_heads, HEAD_DIM),
            s_dfk.reshape(1, nb, t_new, df_heads, 2, HEAD_DIM), s_dfv.reshape(1, nb, t_new, df_heads, 2 * HEAD_DIM),
            s_conv[None])
```
